```python
import jax
import jax.numpy as jnp
from jax import lax
import numpy as np

D_MODEL = 1024
BATCH = 8
SEQ = 4096
DEPTH = 2

GRID_W = 64
N_MEM = 256
NORM_EPS = 1e-6
LN_EPS = 1e-5

D_CONV = D_MODEL
CONV_W = 31

RWKV_HEAD = 64
D_RWKV = D_MODEL
H_RWKV = D_RWKV // RWKV_HEAD
R_DECAY = 64
R_ICL = 64
R_GATE = 128
RWKV_GN_EPS = 1e-5 * RWKV_HEAD

NA_HEAD = 64
D_NA = D_MODEL
H_NA = D_NA // NA_HEAD
NA_WIN_R_MAX = 8
NA_WIN_C = 16

N_BRANCH = 3
XA_HEADS = 4
XA_HEAD = D_MODEL // XA_HEADS
D_FF = 4 * D_MODEL

N_CONV_IN = 2 * D_CONV
N_RWKV_IN = 3 * D_RWKV + 2 * R_DECAY + 2 * R_ICL + R_GATE
N_NA_IN = 3 * D_NA
N_GATE_IN = N_BRANCH * D_MODEL
OFF_RWKV = N_CONV_IN
OFF_NA = OFF_RWKV + N_RWKV_IN
OFF_GATE = OFF_NA + N_NA_IN
N_IN = OFF_GATE + N_GATE_IN

kernel_name = 'hybrid_conv_rwkv7_natten_encoder'


def rms_norm(x, g):
    xf = x.astype(jnp.float32)
    y = xf * lax.rsqrt(jnp.mean(xf * xf, axis=-1, keepdims=True) + NORM_EPS)
    return (y * g.astype(jnp.float32)).astype(x.dtype)


def layer_norm(x, g, b, eps):
    xf = x.astype(jnp.float32)
    mu = jnp.mean(xf, axis=-1, keepdims=True)
    var = jnp.mean(jnp.square(xf - mu), axis=-1, keepdims=True)
    y = (xf - mu) * lax.rsqrt(var + eps)
    return (y * g.astype(jnp.float32) + b.astype(jnp.float32)).astype(x.dtype)


def conformer_conv(u, b_glu, dw_w, dw_b, ln_g, ln_b, proj_w, proj_b):
    u = u + b_glu
    a, gate = u[..., :D_CONV], u[..., D_CONV:]
    z = a * jax.nn.sigmoid(gate)
    z = lax.conv_general_dilated(
        z, dw_w[:, None, :], window_strides=(1,),
        padding=[(CONV_W // 2, CONV_W // 2)],
        dimension_numbers=('NWC', 'WIO', 'NWC'),
        feature_group_count=D_CONV) + dw_b
    z = jax.nn.silu(layer_norm(z, ln_g, ln_b, LN_EPS))
    return z @ proj_w + proj_b


def centred_token_shift(p, mu_prev, mu_next):
    zero = jnp.zeros_like(p[:, :1])
    prev = jnp.concatenate([zero, p[:, :-1]], axis=1)
    nxt = jnp.concatenate([p[:, 1:], zero], axis=1)
    return p + mu_prev * (prev - p) + mu_next * (nxt - p)


def _dirs_to_scan(t_fwd, t_bwd):
    B, S, _ = t_fwd.shape
    t = jnp.stack([t_fwd, t_bwd[:, ::-1]], axis=0).astype(jnp.float32)
    return t.reshape(2, B, S, H_RWKV, RWKV_HEAD).transpose(2, 0, 1, 3, 4)


def _rwkv7_step(state, inp):
    r, w, k, v, a, b = inp
    sa = jnp.einsum('...ij,...j->...i', state, a)
    state = state * w[..., None, :] + sa[..., :, None] * b[..., None, :] + v[..., :, None] * k[..., None, :]
    y = jnp.einsum('...ij,...j->...i', state, r)
    return state, y


def rwkv7_bidirectional(p, mu_prev, mu_next, w0, w2, a0, a2, g2, k_k, k_a, r_k, ln_g, ln_b, proj_w):
    B, S, _ = p.shape
    p = centred_token_shift(p, mu_prev, mu_next)
    i1, i2, i3 = D_RWKV, 2 * D_RWKV, 3 * D_RWKV
    i4 = i3 + 2 * R_DECAY
    i5 = i4 + 2 * R_ICL
    r, k, v = p[..., :i1], p[..., i1:i2], p[..., i2:i3]
    wd = p[..., i3:i4].reshape(B, S, 2, R_DECAY)
    ad = p[..., i4:i5].reshape(B, S, 2, R_ICL)
    gd = p[..., i5:]
    w_log = -jax.nn.softplus(-(w0 + jnp.einsum('bsnr,nrc->bsnc', jnp.tanh(wd), w2)).astype(jnp.float32)) - 0.5
    decay = jnp.exp(-jnp.exp(w_log))
    a = jax.nn.sigmoid(a0 + jnp.einsum('bsnr,nrc->bsnc', ad, a2))
    g = jax.nn.sigmoid(gd) @ g2
    kk = (k * k_k).reshape(B, S, H_RWKV, RWKV_HEAD).astype(jnp.float32)
    kk = kk / jnp.maximum(jnp.sqrt(jnp.sum(kk * kk, axis=-1, keepdims=True)), 1e-12)
    kk = kk.reshape(B, S, D_RWKV).astype(p.dtype)
    k_dir = k[:, :, None, :] * (1.0 + (a - 1.0) * k_a)
    b_dir = kk[:, :, None, :] * a
    xs = (_dirs_to_scan(r, r),
          _dirs_to_scan(decay[:, :, 0], decay[:, :, 1]),
          _dirs_to_scan(k_dir[:, :, 0], k_dir[:, :, 1]),
          _dirs_to_scan(v, v),
          _dirs_to_scan(-kk, -kk),
          _dirs_to_scan(b_dir[:, :, 0], b_dir[:, :, 1]))
    state0 = jnp.zeros((2, B, H_RWKV, RWKV_HEAD, RWKV_HEAD), jnp.float32)
    _, ys = lax.scan(_rwkv7_step, state0, xs)
    y = (ys[:, 0] + ys[::-1, 1]).transpose(1, 0, 2, 3)
    mu = jnp.mean(y, axis=-1, keepdims=True)
    var = jnp.mean(jnp.square(y - mu), axis=-1, keepdims=True)
    y = (y - mu) * lax.rsqrt(var + RWKV_GN_EPS)
    y = y * ln_g.reshape(H_RWKV, RWKV_HEAD).astype(jnp.float32) + ln_b.reshape(H_RWKV, RWKV_HEAD).astype(jnp.float32)
    rk = jnp.sum(r.reshape(B, S, 1, H_RWKV, RWKV_HEAD) * k_dir.reshape(B, S, 2, H_RWKV, RWKV_HEAD) * r_k, axis=(2, 4))
    bonus = rk[..., None] * v.reshape(B, S, H_RWKV, RWKV_HEAD)
    out = (y.astype(p.dtype) + bonus).reshape(B, S, D_RWKV) * g
    return out @ proj_w


def neighbourhood_attention(p, rpb, proj_w):
    B, S, _ = p.shape
    rows = S // GRID_W
    win_r = min(NA_WIN_R_MAX, rows)
    grid = (B, rows, GRID_W, H_NA, NA_HEAD)
    q = (p[..., :D_NA] * NA_HEAD ** -0.5).reshape(grid)
    k = p[..., D_NA:2 * D_NA].reshape(grid)
    v = p[..., 2 * D_NA:].reshape(grid)
    col = jnp.arange(GRID_W)
    c_start = jnp.clip(col - NA_WIN_C // 2, 0, GRID_W - NA_WIN_C)
    col_ok = (col[None, :] >= c_start[:, None]) & (col[None, :] < c_start[:, None] + NA_WIN_C)
    dx_idx = jnp.clip(col[None, :] - col[:, None], 1 - NA_WIN_C, NA_WIN_C - 1) + NA_WIN_C - 1
    rpb_cols = rpb[:, :, dx_idx].astype(jnp.float32)

    def row_block(r):
        r0 = jnp.clip(r - win_r // 2, 0, rows - win_r)
        q_r = lax.dynamic_index_in_dim(q, r, axis=1, keepdims=False)
        k_b = lax.dynamic_slice_in_dim(k, r0, win_r, axis=1)
        v_b = lax.dynamic_slice_in_dim(v, r0, win_r, axis=1)
        dy_idx = r0 + jnp.arange(win_r) - r + NA_WIN_R_MAX - 1
        bias = jnp.take(rpb_cols, dy_idx, axis=1).transpose(0, 2, 1, 3)
        s = jnp.einsum('bchd,bpwhd->bhcpw', q_r, k_b).astype(jnp.float32) + bias[None]
        s = jnp.where(col_ok[None, None, :, None, :], s, -1e30)
        prob = jax.nn.softmax(s.reshape(B, H_NA, GRID_W, win_r * GRID_W), axis=-1)
        prob = prob.reshape(s.shape).astype(v.dtype)
        return jnp.einsum('bhcpw,bpwhd->bchd', prob, v_b)

    out = lax.map(row_block, jnp.arange(rows))
    out = out.transpose(1, 0, 2, 3, 4).reshape(B, S, D_NA)
    return out @ proj_w


def memory_cross_attention(h, mem_n, wq, wkv, wo):
    B, S, _ = h.shape
    M = mem_n.shape[1]
    q = (h @ wq).reshape(B, S, XA_HEADS, XA_HEAD)
    kv = mem_n @ wkv
    km = kv[..., :D_MODEL].reshape(B, M, XA_HEADS, XA_HEAD)
    vm = kv[..., D_MODEL:].reshape(B, M, XA_HEADS, XA_HEAD)
    s = jnp.einsum('bshd,bmhd->bhsm', q, km).astype(jnp.float32) * XA_HEAD ** -0.5
    prob = jax.nn.softmax(s, axis=-1).astype(vm.dtype)
    o = jnp.einsum('bhsm,bmhd->bshd', prob, vm).reshape(B, S, D_MODEL)
    return o @ wo


def setup_inputs(seed: int = 0) -> dict:
    key = jax.random.key(seed)
    ks = iter(jax.random.split(key, 64))
    L, D = DEPTH, D_MODEL

    def nrm(shape, scale):
        return scale * jax.random.normal(next(ks), shape, jnp.float32)

    def gain(shape):
        return 1.0 + nrm(shape, 0.02)

    def unif(shape, lo, hi):
        return jax.random.uniform(next(ks), shape, jnp.float32, lo, hi)

    return {
        'x': nrm((BATCH, SEQ, D), 1.0),
        'mem': nrm((BATCH, N_MEM, D), 1.0),
        'norm_mix_g': gain((L, D)),
        'w_in': nrm((L, D, N_IN), D ** -0.5),
        'gate_b': nrm((L, N_GATE_IN), 0.02),
        'conv_b_glu': nrm((L, N_CONV_IN), 0.02),
        'conv_dw_w': nrm((L, CONV_W, D_CONV), CONV_W ** -0.5),
        'conv_dw_b': nrm((L, D_CONV), 0.02),
        'conv_ln_g': gain((L, D_CONV)),
        'conv_ln_b': nrm((L, D_CONV), 0.02),
        'conv_proj_w': nrm((L, D_CONV, D), D_CONV ** -0.5),
        'conv_proj_b': nrm((L, D), 0.02),
        'rwkv_mu_prev': unif((L, N_RWKV_IN), 0.0, 0.5),
        'rwkv_mu_next': unif((L, N_RWKV_IN), 0.0, 0.5),
        'rwkv_w0': unif((L, 2, D_RWKV), -4.0, 1.0),
        'rwkv_w2': nrm((L, 2, R_DECAY, D_RWKV), 0.3 * R_DECAY ** -0.5),
        'rwkv_a0': nrm((L, 2, D_RWKV), 0.5),
        'rwkv_a2': nrm((L, 2, R_ICL, D_RWKV), 0.5 * R_ICL ** -0.5),
        'rwkv_g2': nrm((L, R_GATE, D_RWKV), R_GATE ** -0.5),
        'rwkv_k_k': 0.85 + nrm((L, D_RWKV), 0.02),
        'rwkv_k_a': gain((L, D_RWKV)),
        'rwkv_r_k': nrm((L, H_RWKV, RWKV_HEAD), 0.1),
        'rwkv_ln_g': gain((L, D_RWKV)),
        'rwkv_ln_b': nrm((L, D_RWKV), 0.02),
        'rwkv_proj_w': nrm((L, D_RWKV, D), D_RWKV ** -0.5),
        'na_rpb': nrm((L, H_NA, 2 * NA_WIN_R_MAX - 1, 2 * NA_WIN_C - 1), 0.1),
        'na_proj_w': nrm((L, D_NA, D), D_NA ** -0.5),
        'w_out': nrm((L, D, D), D ** -0.5),
        'norm_xa_g': gain((L, D)),
        'norm_mem_g': gain((L, D)),
        'xa_wq': nrm((L, D, D), D ** -0.5),
        'xa_wkv': nrm((L, D, 2 * D), D ** -0.5),
        'xa_wo': nrm((L, D, D), D ** -0.5),
        'norm_mlp_g': gain((L, D)),
        'mlp_w1': nrm((L, D, D_FF), D ** -0.5),
        'mlp_w2': nrm((L, D_FF, D), D_FF ** -0.5),
        'norm_f_g': gain((D,)),
    }


def reference(x, mem, norm_mix_g, w_in, gate_b, conv_b_glu, conv_dw_w, conv_dw_b, conv_ln_g, conv_ln_b,
              conv_proj_w, conv_proj_b, rwkv_mu_prev, rwkv_mu_next, rwkv_w0, rwkv_w2, rwkv_a0, rwkv_a2,
              rwkv_g2, rwkv_k_k, rwkv_k_a, rwkv_r_k, rwkv_ln_g, rwkv_ln_b, rwkv_proj_w, na_rpb, na_proj_w,
              w_out, norm_xa_g, norm_mem_g, xa_wq, xa_wkv, xa_wo, norm_mlp_g, mlp_w1, mlp_w2, norm_f_g):
    B, S, _ = x.shape
    for l in range(DEPTH):
        h = rms_norm(x, norm_mix_g[l])
        p = h @ w_in[l]
        y_conv = conformer_conv(p[..., :OFF_RWKV], conv_b_glu[l], conv_dw_w[l], conv_dw_b[l],
                                conv_ln_g[l], conv_ln_b[l], conv_proj_w[l], conv_proj_b[l])
        y_rwkv = rwkv7_bidirectional(p[..., OFF_RWKV:OFF_NA], rwkv_mu_prev[l], rwkv_mu_next[l],
                                     rwkv_w0[l], rwkv_w2[l], rwkv_a0[l], rwkv_a2[l], rwkv_g2[l],
                                     rwkv_k_k[l], rwkv_k_a[l], rwkv_r_k[l], rwkv_ln_g[l], rwkv_ln_b[l],
                                     rwkv_proj_w[l])
        y_na = neighbourhood_attention(p[..., OFF_NA:OFF_GATE], na_rpb[l], na_proj_w[l])
        gates = jax.nn.sigmoid(p[..., OFF_GATE:] + gate_b[l]).reshape(B, S, N_BRANCH, D_MODEL)
        merged = gates[:, :, 0] * y_conv + gates[:, :, 1] * y_rwkv + gates[:, :, 2] * y_na
        x = x + merged @ w_out[l]
        mem_n = rms_norm(mem, norm_mem_g[l])
        x = x + memory_cross_attention(rms_norm(x, norm_xa_g[l]), mem_n, xa_wq[l], xa_wkv[l], xa_wo[l])
        h = rms_norm(x, norm_mlp_g[l])
        x = x + jnp.square(jax.nn.relu(h @ mlp_w1[l])) @ mlp_w2[l]
    return rms_norm(x, norm_f_g)
```

```python
import functools
import math

import jax
import jax.numpy as jnp
from jax import lax
from jax.experimental import pallas as pl
from jax.experimental.pallas import tpu as pltpu

F32 = jnp.float32
BF16 = jnp.bfloat16

NORM_EPS = 1e-6
LN_EPS = 1e-5
GRID_W = 64
CONV_W = 31
RWKV_HEAD = 64
R_DECAY = 64
R_ICL = 64
R_GATE = 128
RWKV_GN_EPS = 1e-5 * RWKV_HEAD
NA_HEAD = 64
NA_WIN_R_MAX = 8
NA_WIN_C = 16
XA_HEADS = 4
LANES = 128
VMEM_LIMIT = 48 * 1024 * 1024
EXP_NEG_HALF = math.exp(-0.5)


def _sig(x):
    return 1.0 / (1.0 + jnp.exp(-x))


def _rms(x, g):
    ms = jnp.mean(x * x, axis=-1, keepdims=True)
    return x * lax.rsqrt(ms + NORM_EPS) * g


def _params(sem):
    return pltpu.CompilerParams(dimension_semantics=sem, vmem_limit_bytes=VMEM_LIMIT)


def _norm_matmul_kernel(x_ref, g_ref, w_ref, b_ref, o_ref, xn_ref, *, act):
    @pl.when(pl.program_id(1) == 0)
    def _():
        xn_ref[...] = _rms(x_ref[...], g_ref[...]).astype(BF16)

    y = jnp.dot(xn_ref[...], w_ref[...], preferred_element_type=F32) + b_ref[...]
    if act == "sigmoid":
        y = _sig(y)
    o_ref[...] = y.astype(o_ref.dtype)


def _norm_matmul(x2, g, w_bf, bias, *, tm, tn, act=None, out_dtype=F32, name):
    M, K = x2.shape
    N = w_bf.shape[1]
    return pl.pallas_call(
        functools.partial(_norm_matmul_kernel, act=act),
        grid=(M // tm, N // tn),
        in_specs=[pl.BlockSpec((tm, K), lambda i, j: (i, 0)),
                  pl.BlockSpec((1, K), lambda i, j: (0, 0)),
                  pl.BlockSpec((K, tn), lambda i, j: (0, j)),
                  pl.BlockSpec((1, tn), lambda i, j: (0, j))],
        out_specs=pl.BlockSpec((tm, tn), lambda i, j: (i, j)),
        out_shape=jax.ShapeDtypeStruct((M, N), out_dtype),
        scratch_shapes=[pltpu.VMEM((tm, K), BF16)],
        compiler_params=_params(("arbitrary", "arbitrary")),
        name=name,
    )(x2, g, w_bf, bias)


CONV_HALO = 16
CONV_ROWS = 32
CONV_LANES = 256


def _conv_kernel(cur_ref, prev_ref, next_ref, bglu_ref, dww_ref, dwb_ref, lng_ref, lnb_ref, o_ref, zp_ref,
                 *, ts, C):
    i = pl.program_id(1)
    n = pl.num_programs(1)
    H = CONV_HALO
    bg = bglu_ref[...]

    def glu(u):
        return (u[:, :C] + bg[:, :C]) * _sig(u[:, C:] + bg[:, C:])

    zp_ref[H:H + ts, :] = glu(cur_ref[0])
    zp_ref[0:H, :] = jnp.where(i > 0, glu(prev_ref[0]), 0.0)
    zp_ref[H + ts:H + ts + H, :] = jnp.where(i < n - 1, glu(next_ref[0]), 0.0)

    R, LC = CONV_ROWS, CONV_LANES
    first = H - CONV_W // 2

    def conv_body(rc, carry):
        r0 = pl.multiple_of(rc * R, R)
        for lc in range(C // LC):
            ls = slice(lc * LC, (lc + 1) * LC)
            win = zp_ref[pl.ds(r0, R + 2 * H), ls]
            acc = jnp.zeros((R, LC), F32)
            for k in range(CONV_W):
                acc = acc + win[first + k:first + k + R, :] * dww_ref[k:k + 1, ls]
            o_ref[0, pl.ds(r0, R), ls] = acc + dwb_ref[:, ls]
        return carry

    lax.fori_loop(0, ts // R, conv_body, 0)

    def ln_body(rc, carry):
        r0 = pl.multiple_of(rc * R, R)
        z = o_ref[0, pl.ds(r0, R), :]
        mu = jnp.mean(z, axis=-1, keepdims=True)
        d = z - mu
        var = jnp.mean(d * d, axis=-1, keepdims=True)
        y = d * lax.rsqrt(var + LN_EPS) * lng_ref[...] + lnb_ref[...]
        o_ref[0, pl.ds(r0, R), :] = y * _sig(y)
        return carry

    lax.fori_loop(0, ts // R, ln_body, 0)


def _conformer_conv(u, b_glu, dw_w, dw_b, ln_g, ln_b, *, ts):
    B, S, C2 = u.shape
    C = C2 // 2
    H = CONV_HALO
    nh = ts // H
    last = S // H - 1
    dww = jnp.zeros((32, C), F32).at[:CONV_W].set(dw_w)
    return pl.pallas_call(
        functools.partial(_conv_kernel, ts=ts, C=C),
        grid=(B, S // ts),
        in_specs=[pl.BlockSpec((1, ts, C2), lambda b, i: (b, i, 0)),
                  pl.BlockSpec((1, H, C2), lambda b, i: (b, jnp.maximum(i * nh - 1, 0), 0)),
                  pl.BlockSpec((1, H, C2), lambda b, i: (b, jnp.minimum((i + 1) * nh, last), 0)),
                  pl.BlockSpec((1, C2), lambda b, i: (0, 0)),
                  pl.BlockSpec((32, C), lambda b, i: (0, 0)),
                  pl.BlockSpec((1, C), lambda b, i: (0, 0)),
                  pl.BlockSpec((1, C), lambda b, i: (0, 0)),
                  pl.BlockSpec((1, C), lambda b, i: (0, 0))],
        out_specs=pl.BlockSpec((1, ts, C), lambda b, i: (b, i, 0)),
        out_shape=jax.ShapeDtypeStruct((B, S, C), F32),
        scratch_shapes=[pltpu.VMEM((ts + 2 * H, C), F32)],
        compiler_params=_params(("arbitrary", "arbitrary")),
        name="conformer_conv",
    )(u, u, u, b_glu[None], dww, dw_b[None], ln_g[None], ln_b[None])


def _rwkv_nat_kernel(cur_ref, prev_ref, next_ref, mup_ref, mun_ref, w0_ref, w2_ref, a0_ref, a2_ref, g2_ref,
                     ka_ref, rk_ref, bd_ref,
                     r_o, k_o, v_o, d0_o, d1_o, s0_o, s1_o, g_o, bon_o, *, Tb, D):
    i = pl.program_id(1)
    n = pl.num_programs(1)
    rows = lax.broadcasted_iota(jnp.int32, (Tb, LANES), 0)
    first_row = rows == 0
    last_row = rows == Tb - 1
    has_prev = i > 0
    has_next = i < n - 1

    def shifted(c0, c1):
        outs = []
        for c in range(c0, c1, LANES):
            cs = slice(c, c + LANES)
            x = cur_ref[0, :, cs]
            prow = jnp.where(has_prev, prev_ref[0, 7:8, cs], 0.0)
            nrow = jnp.where(has_next, next_ref[0, 0:1, cs], 0.0)
            xp = jnp.where(first_row, prow, pltpu.roll(x, 1, 0))
            xn = jnp.where(last_row, nrow, pltpu.roll(x, Tb - 1, 0))
            outs.append(x + mup_ref[:, cs] * (xp - x) + mun_ref[:, cs] * (xn - x))
        return outs[0] if len(outs) == 1 else jnp.concatenate(outs, axis=-1)

    r = shifted(0, D)
    k = shifted(D, 2 * D)
    v = shifted(2 * D, 3 * D)
    wd = shifted(3 * D, 3 * D + 2 * R_DECAY)
    ad = shifted(3 * D + 2 * R_DECAY, 3 * D + 2 * R_DECAY + 2 * R_ICL)
    gd = shifted(3 * D + 2 * R_DECAY + 2 * R_ICL, 3 * D + 2 * R_DECAY + 2 * R_ICL + R_GATE)

    zw = w0_ref[...] + jnp.dot(jnp.tanh(wd).astype(BF16), w2_ref[...], preferred_element_type=F32)
    dec = jnp.exp(-EXP_NEG_HALF * _sig(zw))
    asig = _sig(a0_ref[...] + jnp.dot(ad.astype(BF16), a2_ref[...], preferred_element_type=F32))
    g = jnp.dot(_sig(gd).astype(BF16), g2_ref[...], preferred_element_type=F32)

    s0 = asig[:, :D]
    s1 = asig[:, D:]
    e = r * k * (2.0 + (s0 + s1 - 2.0) * ka_ref[...]) * rk_ref[...]
    parts = []
    for c in range(0, D, LANES):
        et = e[:, c:c + LANES]
        hi = et.astype(BF16)
        lo = (et - hi.astype(F32)).astype(BF16)
        parts.append(jnp.dot(hi, bd_ref[...], preferred_element_type=F32)
                     + jnp.dot(lo, bd_ref[...], preferred_element_type=F32))
    rk_b = jnp.concatenate(parts, axis=-1)

    r_o[0] = r
    k_o[0] = k
    v_o[0] = v
    d0_o[0] = dec[:, :D]
    d1_o[0] = dec[:, D:]
    s0_o[0] = s0
    s1_o[0] = s1
    g_o[0] = g
    bon_o[0] = rk_b * v


def _rwkv_nat(p, mu_prev, mu_next, w0, w2, a0, a2, g2, k_a, r_k, *, Tb):
    B, S, NR = p.shape
    D = (NR - 2 * R_DECAY - 2 * R_ICL - R_GATE) // 3
    nh = Tb // 8
    last = S // 8 - 1
    zero = jnp.zeros((R_DECAY, D), F32)
    w2bd = jnp.concatenate([jnp.concatenate([w2[0], zero], 1), jnp.concatenate([zero, w2[1]], 1)], 0).astype(BF16)
    a2bd = jnp.concatenate([jnp.concatenate([a2[0], zero], 1), jnp.concatenate([zero, a2[1]], 1)], 0).astype(BF16)
    half = jnp.arange(LANES) // RWKV_HEAD
    bd = (half[:, None] == half[None, :]).astype(BF16)
    full = lambda shape: pl.BlockSpec(shape, lambda b, i: (0,) * len(shape))
    out_spec = pl.BlockSpec((1, Tb, D), lambda b, i: (b, i, 0))
    out = jax.ShapeDtypeStruct((B, S, D), F32)
    return pl.pallas_call(
        functools.partial(_rwkv_nat_kernel, Tb=Tb, D=D),
        grid=(B, S // Tb),
        in_specs=[pl.BlockSpec((1, Tb, NR), lambda b, i: (b, i, 0)),
                  pl.BlockSpec((1, 8, NR), lambda b, i: (b, jnp.maximum(i * nh - 1, 0), 0)),
                  pl.BlockSpec((1, 8, NR), lambda b, i: (b, jnp.minimum((i + 1) * nh, last), 0)),
                  full((1, NR)), full((1, NR)),
                  full((1, 2 * D)), full((2 * R_DECAY, 2 * D)),
                  full((1, 2 * D)), full((2 * R_ICL, 2 * D)),
                  full((R_GATE, D)), full((1, D)), full((1, D)), full((LANES, LANES))],
        out_specs=[out_spec] * 9,
        out_shape=[out] * 9,
        compiler_params=_params(("arbitrary", "arbitrary")),
        name="rwkv_tokens",
    )(p, p, p, mu_prev[None], mu_next[None], w0.reshape(1, 2 * D), w2bd, a0.reshape(1, 2 * D), a2bd,
      g2.astype(BF16), k_a[None], r_k.reshape(1, D), bd)


def _rwkv_T_kernel(rf, rm, kf, km, vf, vm, df, dm, sf, sm_, kk_ref, ka_ref,
                   r_o, w_o, kd_o, v_o, a_o, b_o, *, Tb):
    kkT = kk_ref[...]
    kaT = ka_ref[...]
    half = LANES // 2

    def body(s, carry):
        sr = Tb - 1 - s

        def stacked(f, m):
            top = f[:, pl.ds(s, 1), :, :].reshape(half, LANES)
            bot = m[:, pl.ds(sr, 1), :, :].reshape(half, LANES)
            return jnp.concatenate([top, bot], axis=0).T

        rT = stacked(rf, rm)
        kT = stacked(kf, km)
        vT = stacked(vf, vm)
        wT = stacked(df, dm)
        sT = stacked(sf, sm_)
        kk = (kT * kkT).reshape(2, RWKV_HEAD, LANES)
        nrm = jnp.sqrt(jnp.sum(kk * kk, axis=1, keepdims=True))
        kk = (kk / jnp.maximum(nrm, 1e-12)).reshape(LANES, LANES)
        r_o[s] = rT
        w_o[s] = wT
        kd_o[s] = kT * (1.0 + (sT - 1.0) * kaT)
        v_o[s] = vT
        a_o[s] = -kk
        b_o[s] = kk * sT
        return carry

    lax.fori_loop(0, Tb, body, 0)


def _rwkv_T(r, k, v, d0, d1, s0, s1, k_kT, k_aT, *, Tb):
    B, S, D = r.shape
    P = D // LANES
    nC = S // Tb
    v4 = lambda a: a.reshape(B, S, P, LANES)
    fwd = pl.BlockSpec((B, Tb, P, LANES), lambda c: (0, c, 0, 0))
    mir = pl.BlockSpec((B, Tb, P, LANES), lambda c: (0, nC - 1 - c, 0, 0))
    tab = pl.BlockSpec((LANES, LANES), lambda c: (0, 0))
    out_spec = pl.BlockSpec((Tb, LANES, LANES), lambda c: (c, 0, 0))
    out = jax.ShapeDtypeStruct((S, LANES, LANES), F32)
    return pl.pallas_call(
        functools.partial(_rwkv_T_kernel, Tb=Tb),
        grid=(nC,),
        in_specs=[fwd, mir, fwd, mir, fwd, mir, fwd, mir, fwd, mir, tab, tab],
        out_specs=[out_spec] * 6,
        out_shape=[out] * 6,
        compiler_params=_params(("arbitrary",)),
        name="rwkv_operands",
    )(v4(r), v4(r), v4(k), v4(k), v4(v), v4(v), v4(d0), v4(d1), v4(s0), v4(s1), k_kT, k_aT)


def _scan_kernel(r_ref, w_ref, k_ref, v_ref, a_ref, b_ref, y_ref, P_ref, *, Tc):
    @pl.when(pl.program_id(0) == 0)
    def _():
        P_ref[...] = jnp.zeros(P_ref.shape, F32)

    HD = RWKV_HEAD
    zeros = jnp.zeros((HD, LANES), F32)

    def step(s, carry):
        for g in range(LANES // HD):
            def sa_body(jb, sa):
                base = pl.multiple_of(g * HD + jb * 8, 8)
                a8 = a_ref[s, pl.ds(base, 8), :]
                for jj in range(8):
                    sa = sa + P_ref[base + jj] * a8[jj:jj + 1, :]
                return sa

            sa = lax.fori_loop(0, HD // 8, sa_body, zeros)
            v = v_ref[s, g * HD:(g + 1) * HD, :]

            def upd_body(jb, y):
                base = pl.multiple_of(g * HD + jb * 8, 8)
                w8 = w_ref[s, pl.ds(base, 8), :]
                b8 = b_ref[s, pl.ds(base, 8), :]
                k8 = k_ref[s, pl.ds(base, 8), :]
                r8 = r_ref[s, pl.ds(base, 8), :]
                for jj in range(8):
                    pn = (P_ref[base + jj] * w8[jj:jj + 1, :] + sa * b8[jj:jj + 1, :]) + v * k8[jj:jj + 1, :]
                    P_ref[base + jj] = pn
                    y = y + pn * r8[jj:jj + 1, :]
                return y

            y_ref[s, g * HD:(g + 1) * HD, :] = lax.fori_loop(0, HD // 8, upd_body, zeros)
        return carry

    lax.fori_loop(0, Tc, step, 0)


def _rwkv_scan(rT, wT, kdT, vT, aT, bT, *, Tc):
    S = rT.shape[0]
    spec = pl.BlockSpec((Tc, LANES, LANES), lambda c: (c, 0, 0))
    return pl.pallas_call(
        functools.partial(_scan_kernel, Tc=Tc),
        grid=(S // Tc,),
        in_specs=[spec] * 6,
        out_specs=spec,
        out_shape=jax.ShapeDtypeStruct((S, LANES, LANES), F32),
        scratch_shapes=[pltpu.VMEM((LANES, RWKV_HEAD, LANES), F32)],
        compiler_params=_params(("arbitrary",)),
        name="rwkv_scan",
    )(rT, wT, kdT, vT, aT, bT)


def _rwkv_post_kernel(yf_ref, ym_ref, bon_ref, g_ref, lng_ref, lnb_ref, o_ref, *, Tb):
    half = LANES // 2

    def body(s, carry):
        y = yf_ref[s] + pltpu.roll(ym_ref[Tb - 1 - s], half, 1)
        y = y.reshape(2, RWKV_HEAD, LANES)
        mu = jnp.mean(y, axis=1, keepdims=True)
        d = y - mu
        var = jnp.mean(d * d, axis=1, keepdims=True)
        yn = (d * lax.rsqrt(var + RWKV_GN_EPS)).reshape(LANES, LANES) * lng_ref[...] + lnb_ref[...]
        tok = yn.T[:half].reshape(half // 8, 8, LANES)[:, None]
        o_ref[:, pl.ds(s, 1), :, :] = (tok + bon_ref[:, pl.ds(s, 1), :, :]) * g_ref[:, pl.ds(s, 1), :, :]
        return carry

    lax.fori_loop(0, Tb, body, 0)


def _rwkv_post(yT, bonus, g, ln_gT, ln_bT, *, Tb):
    B, S, D = bonus.shape
    P = D // LANES
    nC = S // Tb
    v4 = lambda a: a.reshape(B, S, P, LANES)
    tok = pl.BlockSpec((B, Tb, P, LANES), lambda c: (0, c, 0, 0))
    tab = pl.BlockSpec((LANES, LANES), lambda c: (0, 0))
    out = pl.pallas_call(
        functools.partial(_rwkv_post_kernel, Tb=Tb),
        grid=(nC,),
        in_specs=[pl.BlockSpec((Tb, LANES, LANES), lambda c: (c, 0, 0)),
                  pl.BlockSpec((Tb, LANES, LANES), lambda c: (nC - 1 - c, 0, 0)),
                  tok, tok, tab, tab],
        out_specs=tok,
        out_shape=jax.ShapeDtypeStruct((B, S, P, LANES), F32),
        compiler_params=_params(("arbitrary",)),
        name="rwkv_groupnorm",
    )(yT, yT, v4(bonus), v4(g), ln_gT, ln_bT)
    return out.reshape(B, S, D)


def _pair_table(p):
    pairs = p.shape[0] // LANES
    t = p.reshape(pairs, LANES).T
    return jnp.tile(t, (1, LANES // pairs))


def _na_kernel(q_ref, k_ref, v_ref, bias_ref, o_ref, kb_ref, vb_ref, *, rows, win):
    GW = GRID_W
    kb_ref[...] = k_ref[0].astype(BF16)
    vb_ref[...] = v_ref[0].astype(BF16)
    lane = lax.broadcasted_iota(jnp.int32, (GW, LANES), 1)
    low = lane < NA_HEAD
    qc = lax.broadcasted_iota(jnp.int32, (GW, win * GW), 0)
    kc = lax.broadcasted_iota(jnp.int32, (GW, win * GW), 1) % GW
    cs = jnp.clip(qc - NA_WIN_C // 2, 0, GW - NA_WIN_C)
    ok = (kc >= cs) & (kc < cs + NA_WIN_C)
    scale = NA_HEAD ** -0.5

    def body(r, carry):
        r0 = jnp.clip(r - win // 2, 0, rows - win)
        qs = pl.multiple_of(r * GW, GW)
        ks = pl.multiple_of(r0 * GW, GW)
        q = q_ref[0, pl.ds(qs, GW), :] * scale
        kb = kb_ref[pl.ds(ks, win * GW), :]
        vb = vb_ref[pl.ds(ks, win * GW), :]
        d = r - r0
        outs = []
        for hh in range(2):
            qh = jnp.where(low if hh == 0 else jnp.logical_not(low), q, 0.0).astype(BF16)
            s = lax.dot_general(qh, kb, (((1,), (1,)), ((), ())), preferred_element_type=F32)
            s = jnp.where(ok, s + bias_ref[hh, d], -1e30)
            m = jnp.max(s, axis=-1, keepdims=True)
            p = jnp.exp(s - m)
            l = jnp.sum(p, axis=-1, keepdims=True)
            outs.append(jnp.dot(p.astype(BF16), vb, preferred_element_type=F32) / l)
        o_ref[0, pl.ds(qs, GW), :] = jnp.where(low, outs[0], outs[1])
        return carry

    lax.fori_loop(0, rows, body, 0)


def _na_bias(rpb, win):
    col = jnp.arange(GRID_W)
    dx = jnp.clip(col[None, :] - col[:, None], 1 - NA_WIN_C, NA_WIN_C - 1) + NA_WIN_C - 1
    d = jnp.arange(win)
    p = jnp.arange(win)
    dy = p[None, :] - d[:, None] + NA_WIN_R_MAX - 1
    t = rpb[:, dy][:, :, :, dx]
    t = t.transpose(0, 1, 3, 2, 4)
    return t.reshape(rpb.shape[0], win, GRID_W, win * GRID_W).astype(F32)


def _neighbourhood_attention(p, rpb):
    B, S, D3 = p.shape
    D = D3 // 3
    pairs = D // LANES
    rows = S // GRID_W
    win = min(NA_WIN_R_MAX, rows)
    bias = _na_bias(rpb, win)
    return pl.pallas_call(
        functools.partial(_na_kernel, rows=rows, win=win),
        grid=(B, pairs),
        in_specs=[pl.BlockSpec((1, S, LANES), lambda b, h: (b, 0, h)),
                  pl.BlockSpec((1, S, LANES), lambda b, h: (b, 0, pairs + h)),
                  pl.BlockSpec((1, S, LANES), lambda b, h: (b, 0, 2 * pairs + h)),
                  pl.BlockSpec((2, win, GRID_W, win * GRID_W), lambda b, h: (h, 0, 0, 0))],
        out_specs=pl.BlockSpec((1, S, LANES), lambda b, h: (b, 0, h)),
        out_shape=jax.ShapeDtypeStruct((B, S, D), F32),
        scratch_shapes=[pltpu.VMEM((S, LANES), BF16), pltpu.VMEM((S, LANES), BF16)],
        compiler_params=_params(("arbitrary", "arbitrary")),
        name="neighbourhood_attention",
    )(p, p, p, bias)


def _merge_kernel(x_ref, zc_ref, yr_ref, yn_ref, gate_ref, wc_ref, bc_ref, wr_ref, wn_ref, wo_ref, o_ref):
    D = x_ref.shape[-1]
    yc = jnp.dot(zc_ref[...].astype(BF16), wc_ref[...], preferred_element_type=F32) + bc_ref[...]
    yr = jnp.dot(yr_ref[...].astype(BF16), wr_ref[...], preferred_element_type=F32)
    yn = jnp.dot(yn_ref[...].astype(BF16), wn_ref[...], preferred_element_type=F32)
    merged = gate_ref[:, :D] * yc + gate_ref[:, D:2 * D] * yr + gate_ref[:, 2 * D:] * yn
    o_ref[...] = x_ref[...] + jnp.dot(merged.astype(BF16), wo_ref[...], preferred_element_type=F32)


def _merge(x2, zc, yr, yn, gate, wc, bc, wr, wn, wo, *, tm):
    M, D = x2.shape
    row = pl.BlockSpec((tm, D), lambda i: (i, 0))
    wsp = pl.BlockSpec((D, D), lambda i: (0, 0))
    return pl.pallas_call(
        _merge_kernel,
        grid=(M // tm,),
        in_specs=[row, row, row, row, pl.BlockSpec((tm, 3 * D), lambda i: (i, 0)),
                  wsp, pl.BlockSpec((1, D), lambda i: (0, 0)), wsp, wsp, wsp],
        out_specs=row,
        out_shape=jax.ShapeDtypeStruct((M, D), F32),
        compiler_params=_params(("arbitrary",)),
        name="gated_merge",
    )(x2, zc, yr, yn, gate, wc, bc, wr, wn, wo)


def _xattn_kernel(x_ref, g_ref, wq_ref, kv_ref, wo_ref, o_ref):
    x = x_ref[0]
    D = x.shape[-1]
    dh = D // XA_HEADS
    h = _rms(x, g_ref[...]).astype(BF16)
    q = jnp.dot(h, wq_ref[...], preferred_element_type=F32)
    outs = []
    for hd in range(XA_HEADS):
        qh = q[:, hd * dh:(hd + 1) * dh].astype(BF16)
        km = kv_ref[0, :, hd * dh:(hd + 1) * dh]
        vm = kv_ref[0, :, D + hd * dh:D + (hd + 1) * dh]
        s = lax.dot_general(qh, km, (((1,), (1,)), ((), ())), preferred_element_type=F32) * dh ** -0.5
        m = jnp.max(s, axis=-1, keepdims=True)
        p = jnp.exp(s - m)
        l = jnp.sum(p, axis=-1, keepdims=True)
        outs.append(jnp.dot(p.astype(BF16), vm, preferred_element_type=F32) / l)
    o = jnp.concatenate(outs, axis=-1).astype(BF16)
    o_ref[0] = x + jnp.dot(o, wo_ref[...], preferred_element_type=F32)


def _cross_attention(x, g, wq, kv, wo, *, tm):
    B, S, D = x.shape
    Mm = kv.shape[1]
    row = pl.BlockSpec((1, tm, D), lambda b, i: (b, i, 0))
    wsp = pl.BlockSpec((D, D), lambda b, i: (0, 0))
    return pl.pallas_call(
        _xattn_kernel,
        grid=(B, S // tm),
        in_specs=[row, pl.BlockSpec((1, D), lambda b, i: (0, 0)), wsp,
                  pl.BlockSpec((1, Mm, 2 * D), lambda b, i: (b, 0, 0)), wsp],
        out_specs=row,
        out_shape=jax.ShapeDtypeStruct((B, S, D), F32),
        compiler_params=_params(("arbitrary", "arbitrary")),
        name="memory_cross_attention",
    )(x, g, wq, kv, wo)


def _mlp_kernel(x_ref, g_ref, w1_ref, w2_ref, gf_ref, o_ref, xn_ref, acc_ref, *, final_norm):
    f = pl.program_id(1)

    @pl.when(f == 0)
    def _():
        xn_ref[...] = _rms(x_ref[...], g_ref[...]).astype(BF16)
        acc_ref[...] = jnp.zeros(acc_ref.shape, F32)

    h = jnp.maximum(jnp.dot(xn_ref[...], w1_ref[...], preferred_element_type=F32), 0.0)
    acc_ref[...] += jnp.dot((h * h).astype(BF16), w2_ref[...], preferred_element_type=F32)

    @pl.when(f == pl.num_programs(1) - 1)
    def _():
        y = x_ref[...] + acc_ref[...]
        if final_norm:
            y = _rms(y, gf_ref[...])
        o_ref[...] = y


def _mlp(x2, g, w1, w2, gf, *, tm, tf, final_norm):
    M, D = x2.shape
    F = w1.shape[1]
    row = pl.BlockSpec((tm, D), lambda i, f: (i, 0))
    vec = pl.BlockSpec((1, D), lambda i, f: (0, 0))
    return pl.pallas_call(
        functools.partial(_mlp_kernel, final_norm=final_norm),
        grid=(M // tm, F // tf),
        in_specs=[row, vec, pl.BlockSpec((D, tf), lambda i, f: (0, f)),
                  pl.BlockSpec((tf, D), lambda i, f: (f, 0)), vec],
        out_specs=row,
        out_shape=jax.ShapeDtypeStruct((M, D), F32),
        scratch_shapes=[pltpu.VMEM((tm, D), BF16), pltpu.VMEM((tm, D), F32)],
        compiler_params=_params(("arbitrary", "arbitrary")),
        name="relu2_mlp",
    )(x2, g, w1, w2, gf)


def _tiles(S):
    return dict(
        proj_tm=512,
        conv_ts=min(512, S),
        nat_tb=min(128, S),
        op_tb=16,
        scan_tc=16,
        merge_tm=256,
        xa_tm=min(512, S),
        mlp_tm=512,
        mlp_tf=1024,
    )


def _pick_tn(n, cap=1280):
    best = LANES
    for t in range(LANES, cap + 1, LANES):
        if n % t == 0:
            best = t
    return best


def kernel(x, mem, norm_mix_g, w_in, gate_b, conv_b_glu, conv_dw_w, conv_dw_b, conv_ln_g, conv_ln_b,
           conv_proj_w, conv_proj_b, rwkv_mu_prev, rwkv_mu_next, rwkv_w0, rwkv_w2, rwkv_a0, rwkv_a2,
           rwkv_g2, rwkv_k_k, rwkv_k_a, rwkv_r_k, rwkv_ln_g, rwkv_ln_b, rwkv_proj_w, na_rpb, na_proj_w,
           w_out, norm_xa_g, norm_mem_g, xa_wq, xa_wkv, xa_wo, norm_mlp_g, mlp_w1, mlp_w2, norm_f_g):
    B, S, D = x.shape
    depth = w_in.shape[0]
    Mm = mem.shape[1]
    T = _tiles(S)
    n_conv = 2 * D
    n_rwkv = 3 * D + 2 * R_DECAY + 2 * R_ICL + R_GATE
    n_na = 3 * D
    off_rwkv = n_conv
    off_na = off_rwkv + n_rwkv
    off_gate = off_na + n_na

    x2 = x.reshape(B * S, D)
    mem2 = mem.reshape(B * Mm, D)
    for l in range(depth):
        w_in_bf = w_in[l].astype(BF16)
        g_mix = norm_mix_g[l][None]

        def proj(c0, c1, name, bias=None, act=None):
            n = c1 - c0
            b = jnp.zeros((1, n), F32) if bias is None else bias[None]
            return _norm_matmul(x2, g_mix, w_in_bf[:, c0:c1], b, tm=T["proj_tm"], tn=_pick_tn(n), act=act,
                                name=name)

        p_conv = proj(0, off_rwkv, "proj_conv").reshape(B, S, n_conv)
        p_rwkv = proj(off_rwkv, off_na, "proj_rwkv").reshape(B, S, n_rwkv)
        p_na = proj(off_na, off_gate, "proj_na").reshape(B, S, n_na)
        gates = proj(off_gate, off_gate + 3 * D, "proj_gate", bias=gate_b[l], act="sigmoid")

        zc = _conformer_conv(p_conv, conv_b_glu[l], conv_dw_w[l], conv_dw_b[l], conv_ln_g[l], conv_ln_b[l],
                             ts=T["conv_ts"])

        r, k, v, d0, d1, s0, s1, g, bonus = _rwkv_nat(
            p_rwkv, rwkv_mu_prev[l], rwkv_mu_next[l], rwkv_w0[l], rwkv_w2[l], rwkv_a0[l], rwkv_a2[l],
            rwkv_g2[l], rwkv_k_a[l], rwkv_r_k[l], Tb=T["nat_tb"])
        ops = _rwkv_T(r, k, v, d0, d1, s0, s1, _pair_table(rwkv_k_k[l]), _pair_table(rwkv_k_a[l]),
                      Tb=T["op_tb"])
        yT = _rwkv_scan(*ops, Tc=T["scan_tc"])
        y_rwkv = _rwkv_post(yT, bonus, g, _pair_table(rwkv_ln_g[l]), _pair_table(rwkv_ln_b[l]), Tb=T["op_tb"])

        y_na = _neighbourhood_attention(p_na, na_rpb[l])

        x2 = _merge(x2, zc.reshape(B * S, D), y_rwkv.reshape(B * S, D), y_na.reshape(B * S, D), gates,
                    conv_proj_w[l].astype(BF16), conv_proj_b[l][None], rwkv_proj_w[l].astype(BF16),
                    na_proj_w[l].astype(BF16), w_out[l].astype(BF16), tm=T["merge_tm"])

        kv = _norm_matmul(mem2, norm_mem_g[l][None], xa_wkv[l].astype(BF16), jnp.zeros((1, 2 * D), F32),
                          tm=min(512, B * Mm), tn=1024, out_dtype=BF16, name="proj_mem_kv")
        xa = _cross_attention(x2.reshape(B, S, D), norm_xa_g[l][None], xa_wq[l].astype(BF16),
                              kv.reshape(B, Mm, 2 * D), xa_wo[l].astype(BF16), tm=T["xa_tm"])

        x2 = _mlp(xa.reshape(B * S, D), norm_mlp_g[l][None], mlp_w1[l].astype(BF16), mlp_w2[l].astype(BF16),
                  norm_f_g[None], tm=T["mlp_tm"], tf=T["mlp_tf"], final_norm=(l == depth - 1))
    return x2.reshape(B, S, D)
```

```python
import functools
import math

import jax
import jax.numpy as jnp
from jax import lax
from jax.experimental import pallas as pl
from jax.experimental.pallas import tpu as pltpu

F32 = jnp.float32
BF16 = jnp.bfloat16

NORM_EPS = 1e-6
LN_EPS = 1e-5
GRID_W = 64
CONV_W = 31
RWKV_HEAD = 64
R_DECAY = 64
R_ICL = 64
R_GATE = 128
RWKV_GN_EPS = 1e-5 * RWKV_HEAD
NA_HEAD = 64
NA_WIN_R_MAX = 8
NA_WIN_C = 16
XA_HEADS = 4
LANES = 128
VMEM_LIMIT = 48 * 1024 * 1024
EXP_NEG_HALF = math.exp(-0.5)


def _sig(x):
    return 1.0 / (1.0 + jnp.exp(-x))


def _rms(x, g):
    ms = jnp.mean(x * x, axis=-1, keepdims=True)
    return x * lax.rsqrt(ms + NORM_EPS) * g


def _params(sem):
    return pltpu.CompilerParams(dimension_semantics=sem, vmem_limit_bytes=VMEM_LIMIT)


def _norm_matmul_kernel(x_ref, g_ref, w_ref, b_ref, o_ref, xn_ref, *, act):
    @pl.when(pl.program_id(1) == 0)
    def _():
        xn_ref[...] = _rms(x_ref[...], g_ref[...]).astype(BF16)

    y = jnp.dot(xn_ref[...], w_ref[...], preferred_element_type=F32) + b_ref[...]
    if act == "sigmoid":
        y = _sig(y)
    o_ref[...] = y.astype(o_ref.dtype)


def _norm_matmul(x2, g, w_bf, bias, *, tm, tn, act=None, out_dtype=F32, name):
    M, K = x2.shape
    N = w_bf.shape[1]
    return pl.pallas_call(
        functools.partial(_norm_matmul_kernel, act=act),
        grid=(M // tm, N // tn),
        in_specs=[pl.BlockSpec((tm, K), lambda i, j: (i, 0)),
                  pl.BlockSpec((1, K), lambda i, j: (0, 0)),
                  pl.BlockSpec((K, tn), lambda i, j: (0, j)),
                  pl.BlockSpec((1, tn), lambda i, j: (0, j))],
        out_specs=pl.BlockSpec((tm, tn), lambda i, j: (i, j)),
        out_shape=jax.ShapeDtypeStruct((M, N), out_dtype),
        scratch_shapes=[pltpu.VMEM((tm, K), BF16)],
        compiler_params=_params(("arbitrary", "arbitrary")),
        name=name,
    )(x2, g, w_bf, bias)


CONV_HALO = 16
CONV_ROWS = 32
CONV_LANES = 256


def _conv_kernel(cur_ref, prev_ref, next_ref, bglu_ref, dww_ref, dwb_ref, lng_ref, lnb_ref, o_ref, zp_ref,
                 *, ts, C):
    i = pl.program_id(1)
    n = pl.num_programs(1)
    H = CONV_HALO
    bg = bglu_ref[...]

    def glu(u):
        return (u[:, :C] + bg[:, :C]) * _sig(u[:, C:] + bg[:, C:])

    zp_ref[H:H + ts, :] = glu(cur_ref[0])
    zp_ref[0:H, :] = jnp.where(i > 0, glu(prev_ref[0]), 0.0)
    zp_ref[H + ts:H + ts + H, :] = jnp.where(i < n - 1, glu(next_ref[0]), 0.0)

    R, LC = CONV_ROWS, CONV_LANES
    first = H - CONV_W // 2

    def conv_body(rc, carry):
        r0 = pl.multiple_of(rc * R, R)
        for lc in range(C // LC):
            ls = slice(lc * LC, (lc + 1) * LC)
            win = zp_ref[pl.ds(r0, R + 2 * H), ls]
            acc = jnp.zeros((R, LC), F32) + dwb_ref[:, ls]
            for rho in range(8):
                part = None
                for k in range(CONV_W):
                    if (first + k) % 8 != rho:
                        continue
                    term = win[first + k:first + k + R, :] * dww_ref[k:k + 1, ls]
                    part = term if part is None else part + term
                if part is not None:
                    acc = acc + part
            o_ref[0, pl.ds(r0, R), ls] = acc
        return carry

    lax.fori_loop(0, ts // R, conv_body, 0)

    def ln_body(rc, carry):
        r0 = pl.multiple_of(rc * R, R)
        z = o_ref[0, pl.ds(r0, R), :]
        mu = jnp.mean(z, axis=-1, keepdims=True)
        d = z - mu
        var = jnp.mean(d * d, axis=-1, keepdims=True)
        y = d * lax.rsqrt(var + LN_EPS) * lng_ref[...] + lnb_ref[...]
        o_ref[0, pl.ds(r0, R), :] = y * _sig(y)
        return carry

    lax.fori_loop(0, ts // R, ln_body, 0)


def _conformer_conv(u, b_glu, dw_w, dw_b, ln_g, ln_b, *, ts):
    B, S, C2 = u.shape
    C = C2 // 2
    H = CONV_HALO
    nh = ts // H
    last = S // H - 1
    dww = jnp.zeros((32, C), F32).at[:CONV_W].set(dw_w)
    return pl.pallas_call(
        functools.partial(_conv_kernel, ts=ts, C=C),
        grid=(B, S // ts),
        in_specs=[pl.BlockSpec((1, ts, C2), lambda b, i: (b, i, 0)),
                  pl.BlockSpec((1, H, C2), lambda b, i: (b, jnp.maximum(i * nh - 1, 0), 0)),
                  pl.BlockSpec((1, H, C2), lambda b, i: (b, jnp.minimum((i + 1) * nh, last), 0)),
                  pl.BlockSpec((1, C2), lambda b, i: (0, 0)),
                  pl.BlockSpec((32, C), lambda b, i: (0, 0)),
                  pl.BlockSpec((1, C), lambda b, i: (0, 0)),
                  pl.BlockSpec((1, C), lambda b, i: (0, 0)),
                  pl.BlockSpec((1, C), lambda b, i: (0, 0))],
        out_specs=pl.BlockSpec((1, ts, C), lambda b, i: (b, i, 0)),
        out_shape=jax.ShapeDtypeStruct((B, S, C), F32),
        scratch_shapes=[pltpu.VMEM((ts + 2 * H, C), F32)],
        compiler_params=_params(("arbitrary", "arbitrary")),
        name="conformer_conv",
    )(u, u, u, b_glu[None], dww, dw_b[None], ln_g[None], ln_b[None])


def _rwkv_nat_kernel(cur_ref, prev_ref, next_ref, mup_ref, mun_ref, w0_ref, w2_ref, a0_ref, a2_ref, g2_ref,
                     ka_ref, rk_ref, bd_ref,
                     r_o, k_o, v_o, d0_o, d1_o, s0_o, s1_o, g_o, bon_o, *, Tb, D):
    i = pl.program_id(1)
    n = pl.num_programs(1)
    rows = lax.broadcasted_iota(jnp.int32, (Tb, LANES), 0)
    first_row = rows == 0
    last_row = rows == Tb - 1
    has_prev = i > 0
    has_next = i < n - 1

    def shifted(c0, c1):
        outs = []
        for c in range(c0, c1, LANES):
            cs = slice(c, c + LANES)
            x = cur_ref[0, :, cs]
            prow = jnp.where(has_prev, prev_ref[0, 7:8, cs], 0.0)
            nrow = jnp.where(has_next, next_ref[0, 0:1, cs], 0.0)
            xp = jnp.where(first_row, prow, pltpu.roll(x, 1, 0))
            xn = jnp.where(last_row, nrow, pltpu.roll(x, Tb - 1, 0))
            outs.append(x + mup_ref[:, cs] * (xp - x) + mun_ref[:, cs] * (xn - x))
        return outs[0] if len(outs) == 1 else jnp.concatenate(outs, axis=-1)

    r = shifted(0, D)
    k = shifted(D, 2 * D)
    v = shifted(2 * D, 3 * D)
    wd = shifted(3 * D, 3 * D + 2 * R_DECAY)
    ad = shifted(3 * D + 2 * R_DECAY, 3 * D + 2 * R_DECAY + 2 * R_ICL)
    gd = shifted(3 * D + 2 * R_DECAY + 2 * R_ICL, 3 * D + 2 * R_DECAY + 2 * R_ICL + R_GATE)

    zw = w0_ref[...] + jnp.dot(jnp.tanh(wd).astype(BF16), w2_ref[...], preferred_element_type=F32)
    dec = jnp.exp(-EXP_NEG_HALF * _sig(zw))
    asig = _sig(a0_ref[...] + jnp.dot(ad.astype(BF16), a2_ref[...], preferred_element_type=F32))
    g = jnp.dot(_sig(gd).astype(BF16), g2_ref[...], preferred_element_type=F32)

    s0 = asig[:, :D]
    s1 = asig[:, D:]
    e = r * k * (2.0 + (s0 + s1 - 2.0) * ka_ref[...]) * rk_ref[...]
    parts = []
    for c in range(0, D, LANES):
        et = e[:, c:c + LANES]
        hi = et.astype(BF16)
        lo = (et - hi.astype(F32)).astype(BF16)
        parts.append(jnp.dot(hi, bd_ref[...], preferred_element_type=F32)
                     + jnp.dot(lo, bd_ref[...], preferred_element_type=F32))
    rk_b = jnp.concatenate(parts, axis=-1)

    r_o[0] = r
    k_o[0] = k
    v_o[0] = v
    d0_o[0] = dec[:, :D]
    d1_o[0] = dec[:, D:]
    s0_o[0] = s0
    s1_o[0] = s1
    g_o[0] = g
    bon_o[0] = rk_b * v


def _rwkv_nat(p, mu_prev, mu_next, w0, w2, a0, a2, g2, k_a, r_k, *, Tb):
    B, S, NR = p.shape
    D = (NR - 2 * R_DECAY - 2 * R_ICL - R_GATE) // 3
    nh = Tb // 8
    last = S // 8 - 1
    zero = jnp.zeros((R_DECAY, D), F32)
    w2bd = jnp.concatenate([jnp.concatenate([w2[0], zero], 1), jnp.concatenate([zero, w2[1]], 1)], 0).astype(BF16)
    a2bd = jnp.concatenate([jnp.concatenate([a2[0], zero], 1), jnp.concatenate([zero, a2[1]], 1)], 0).astype(BF16)
    half = jnp.arange(LANES) // RWKV_HEAD
    bd = (half[:, None] == half[None, :]).astype(BF16)
    full = lambda shape: pl.BlockSpec(shape, lambda b, i: (0,) * len(shape))
    out_spec = pl.BlockSpec((1, Tb, D), lambda b, i: (b, i, 0))
    out = jax.ShapeDtypeStruct((B, S, D), F32)
    return pl.pallas_call(
        functools.partial(_rwkv_nat_kernel, Tb=Tb, D=D),
        grid=(B, S // Tb),
        in_specs=[pl.BlockSpec((1, Tb, NR), lambda b, i: (b, i, 0)),
                  pl.BlockSpec((1, 8, NR), lambda b, i: (b, jnp.maximum(i * nh - 1, 0), 0)),
                  pl.BlockSpec((1, 8, NR), lambda b, i: (b, jnp.minimum((i + 1) * nh, last), 0)),
                  full((1, NR)), full((1, NR)),
                  full((1, 2 * D)), full((2 * R_DECAY, 2 * D)),
                  full((1, 2 * D)), full((2 * R_ICL, 2 * D)),
                  full((R_GATE, D)), full((1, D)), full((1, D)), full((LANES, LANES))],
        out_specs=[out_spec] * 9,
        out_shape=[out] * 9,
        compiler_params=_params(("arbitrary", "arbitrary")),
        name="rwkv_tokens",
    )(p, p, p, mu_prev[None], mu_next[None], w0.reshape(1, 2 * D), w2bd, a0.reshape(1, 2 * D), a2bd,
      g2.astype(BF16), k_a[None], r_k.reshape(1, D), bd)


def _rwkv_T_kernel(rf, rm, kf, km, vf, vm, df, dm, sf, sm_, kk_ref, ka_ref,
                   r_o, w_o, kd_o, v_o, a_o, b_o, *, Tb):
    kkT = kk_ref[...]
    kaT = ka_ref[...]
    half = LANES // 2

    def body(s, carry):
        sr = Tb - 1 - s

        def stacked(f, m):
            top = f[:, pl.ds(s, 1), :, :].reshape(half, LANES)
            bot = m[:, pl.ds(sr, 1), :, :].reshape(half, LANES)
            return jnp.concatenate([top, bot], axis=0).T

        rT = stacked(rf, rm)
        kT = stacked(kf, km)
        vT = stacked(vf, vm)
        wT = stacked(df, dm)
        sT = stacked(sf, sm_)
        kk = (kT * kkT).reshape(2, RWKV_HEAD, LANES)
        nrm = jnp.sqrt(jnp.sum(kk * kk, axis=1, keepdims=True))
        kk = (kk / jnp.maximum(nrm, 1e-12)).reshape(LANES, LANES)
        r_o[s] = rT
        w_o[s] = wT
        kd_o[s] = kT * (1.0 + (sT - 1.0) * kaT)
        v_o[s] = vT
        a_o[s] = -kk
        b_o[s] = kk * sT
        return carry

    lax.fori_loop(0, Tb, body, 0, unroll=2)


def _rwkv_T(r, k, v, d0, d1, s0, s1, k_kT, k_aT, *, Tb):
    B, S, D = r.shape
    P = D // LANES
    nC = S // Tb
    v4 = lambda a: a.reshape(B, S, P, LANES)
    fwd = pl.BlockSpec((B, Tb, P, LANES), lambda c: (0, c, 0, 0))
    mir = pl.BlockSpec((B, Tb, P, LANES), lambda c: (0, nC - 1 - c, 0, 0))
    tab = pl.BlockSpec((LANES, LANES), lambda c: (0, 0))
    out_spec = pl.BlockSpec((Tb, LANES, LANES), lambda c: (c, 0, 0))
    out = jax.ShapeDtypeStruct((S, LANES, LANES), F32)
    return pl.pallas_call(
        functools.partial(_rwkv_T_kernel, Tb=Tb),
        grid=(nC,),
        in_specs=[fwd, mir, fwd, mir, fwd, mir, fwd, mir, fwd, mir, tab, tab],
        out_specs=[out_spec] * 6,
        out_shape=[out] * 6,
        compiler_params=_params(("arbitrary",)),
        name="rwkv_operands",
    )(v4(r), v4(r), v4(k), v4(k), v4(v), v4(v), v4(d0), v4(d1), v4(s0), v4(s1), k_kT, k_aT)


SCAN_I = 64
SCAN_PLANES = 32
SCAN_ACC = 2


def _scan_kernel(r_ref, w_ref, k_ref, v_ref, a_ref, b_ref, y_ref, P_ref, *, Tc):
    @pl.when(pl.program_id(0) == 0)
    def _():
        P_ref[...] = jnp.zeros(P_ref.shape, F32)

    HD = RWKV_HEAD
    zeros = jnp.zeros((SCAN_I, LANES), F32)

    def bcast(ref, s, row):
        return ref[s, pl.ds(row, SCAN_I, stride=0), :]

    def step(s, carry):
        for g in range(LANES // HD):
            for i0 in range(0, HD, SCAN_I):
                isl = slice(i0, i0 + SCAN_I)

                def sa_body(jb, sas):
                    sas = list(sas)
                    for jj in range(SCAN_PLANES):
                        row = g * HD + jb * SCAN_PLANES + jj
                        sas[jj % SCAN_ACC] = sas[jj % SCAN_ACC] + P_ref[row, isl, :] * bcast(a_ref, s, row)
                    return tuple(sas)

                sa = sum(lax.fori_loop(0, HD // SCAN_PLANES, sa_body, (zeros,) * SCAN_ACC))
                v = v_ref[s, g * HD + i0:g * HD + i0 + SCAN_I, :]

                def upd_body(jb, ys):
                    ys = list(ys)
                    for jj in range(SCAN_PLANES):
                        row = g * HD + jb * SCAN_PLANES + jj
                        pn = (P_ref[row, isl, :] * bcast(w_ref, s, row) + sa * bcast(b_ref, s, row)) \
                            + v * bcast(k_ref, s, row)
                        P_ref[row, isl, :] = pn
                        ys[jj % SCAN_ACC] = ys[jj % SCAN_ACC] + pn * bcast(r_ref, s, row)
                    return tuple(ys)

                y = sum(lax.fori_loop(0, HD // SCAN_PLANES, upd_body, (zeros,) * SCAN_ACC))
                y_ref[s, g * HD + i0:g * HD + i0 + SCAN_I, :] = y
        return carry

    lax.fori_loop(0, Tc, step, 0)


def _rwkv_scan(rT, wT, kdT, vT, aT, bT, *, Tc):
    S = rT.shape[0]
    spec = pl.BlockSpec((Tc, LANES, LANES), lambda c: (c, 0, 0))
    return pl.pallas_call(
        functools.partial(_scan_kernel, Tc=Tc),
        grid=(S // Tc,),
        in_specs=[spec] * 6,
        out_specs=spec,
        out_shape=jax.ShapeDtypeStruct((S, LANES, LANES), F32),
        scratch_shapes=[pltpu.VMEM((LANES, RWKV_HEAD, LANES), F32)],
        compiler_params=_params(("arbitrary",)),
        name="rwkv_scan",
    )(rT, wT, kdT, vT, aT, bT)


def _rwkv_post_kernel(yf_ref, ym_ref, bon_ref, g_ref, lng_ref, lnb_ref, o_ref, *, Tb):
    half = LANES // 2

    def body(s, carry):
        y = yf_ref[s] + pltpu.roll(ym_ref[Tb - 1 - s], half, 1)
        y = y.reshape(2, RWKV_HEAD, LANES)
        mu = jnp.mean(y, axis=1, keepdims=True)
        d = y - mu
        var = jnp.mean(d * d, axis=1, keepdims=True)
        yn = (d * lax.rsqrt(var + RWKV_GN_EPS)).reshape(LANES, LANES) * lng_ref[...] + lnb_ref[...]
        tok = yn.T[:half].reshape(half // 8, 8, LANES)[:, None]
        o_ref[:, pl.ds(s, 1), :, :] = (tok + bon_ref[:, pl.ds(s, 1), :, :]) * g_ref[:, pl.ds(s, 1), :, :]
        return carry

    lax.fori_loop(0, Tb, body, 0, unroll=2)


def _rwkv_post(yT, bonus, g, ln_gT, ln_bT, *, Tb):
    B, S, D = bonus.shape
    P = D // LANES
    nC = S // Tb
    v4 = lambda a: a.reshape(B, S, P, LANES)
    tok = pl.BlockSpec((B, Tb, P, LANES), lambda c: (0, c, 0, 0))
    tab = pl.BlockSpec((LANES, LANES), lambda c: (0, 0))
    out = pl.pallas_call(
        functools.partial(_rwkv_post_kernel, Tb=Tb),
        grid=(nC,),
        in_specs=[pl.BlockSpec((Tb, LANES, LANES), lambda c: (c, 0, 0)),
                  pl.BlockSpec((Tb, LANES, LANES), lambda c: (nC - 1 - c, 0, 0)),
                  tok, tok, tab, tab],
        out_specs=tok,
        out_shape=jax.ShapeDtypeStruct((B, S, P, LANES), F32),
        compiler_params=_params(("arbitrary",)),
        name="rwkv_groupnorm",
    )(yT, yT, v4(bonus), v4(g), ln_gT, ln_bT)
    return out.reshape(B, S, D)


def _pair_table(p):
    pairs = p.shape[0] // LANES
    t = p.reshape(pairs, LANES).T
    return jnp.tile(t, (1, LANES // pairs))


NA_GROUP = 4


def _na_kernel(q_ref, k_ref, v_ref, bias_ref, o_ref, kb_ref, vb_ref, *, rows, win):
    GW = GRID_W
    kb_ref[...] = k_ref[0].astype(BF16)
    vb_ref[...] = v_ref[0].astype(BF16)
    lane = lax.broadcasted_iota(jnp.int32, (2 * GW, LANES), 1)
    head = lax.broadcasted_iota(jnp.int32, (2 * GW, LANES), 0) // GW
    own = (lane // NA_HEAD) == head
    qc = lax.broadcasted_iota(jnp.int32, (2 * GW, win * GW), 0) % GW
    kc = lax.broadcasted_iota(jnp.int32, (2 * GW, win * GW), 1) % GW
    cs = jnp.clip(qc - NA_WIN_C // 2, 0, GW - NA_WIN_C)
    ok = (kc >= cs) & (kc < cs + NA_WIN_C)
    low = lax.broadcasted_iota(jnp.int32, (GW, LANES), 1) < NA_HEAD
    scale = NA_HEAD ** -0.5

    def body(rg, carry):
        scores, vbs, qss = [], [], []
        for i in range(NA_GROUP):
            r = rg * NA_GROUP + i
            r0 = jnp.clip(r - win // 2, 0, rows - win)
            qs = pl.multiple_of(r * GW, GW)
            ks = pl.multiple_of(r0 * GW, GW)
            q = q_ref[0, pl.ds(qs, GW), :] * scale
            qq = jnp.where(own, jnp.concatenate([q, q], axis=0), 0.0).astype(BF16)
            kb = kb_ref[pl.ds(ks, win * GW), :]
            s = lax.dot_general(qq, kb, (((1,), (1,)), ((), ())), preferred_element_type=F32)
            scores.append(jnp.where(ok, s + bias_ref[0, r - r0], -1e30))
            vbs.append(vb_ref[pl.ds(ks, win * GW), :])
            qss.append(qs)
        probs, sums = [], []
        for s in scores:
            p = jnp.exp(s - jnp.max(s, axis=-1, keepdims=True))
            sums.append(jnp.sum(p, axis=-1, keepdims=True))
            probs.append(p.astype(BF16))
        for p, l, vb, qs in zip(probs, sums, vbs, qss):
            o = jnp.dot(p, vb, preferred_element_type=F32) / l
            o_ref[0, pl.ds(qs, GW), :] = jnp.where(low, o[:GW], o[GW:])
        return carry

    lax.fori_loop(0, rows // NA_GROUP, body, 0)


def _na_bias(rpb, win):
    col = jnp.arange(GRID_W)
    dx = jnp.clip(col[None, :] - col[:, None], 1 - NA_WIN_C, NA_WIN_C - 1) + NA_WIN_C - 1
    d = jnp.arange(win)
    p = jnp.arange(win)
    dy = p[None, :] - d[:, None] + NA_WIN_R_MAX - 1
    t = rpb[:, dy][:, :, :, dx]
    H = rpb.shape[0]
    t = t.reshape(H // 2, 2, win, win, GRID_W, GRID_W).transpose(0, 2, 1, 4, 3, 5)
    return t.reshape(H // 2, win, 2 * GRID_W, win * GRID_W).astype(F32)


def _neighbourhood_attention(p, rpb):
    B, S, D3 = p.shape
    D = D3 // 3
    pairs = D // LANES
    rows = S // GRID_W
    win = min(NA_WIN_R_MAX, rows)
    bias = _na_bias(rpb, win)
    return pl.pallas_call(
        functools.partial(_na_kernel, rows=rows, win=win),
        grid=(B, pairs),
        in_specs=[pl.BlockSpec((1, S, LANES), lambda b, h: (b, 0, h)),
                  pl.BlockSpec((1, S, LANES), lambda b, h: (b, 0, pairs + h)),
                  pl.BlockSpec((1, S, LANES), lambda b, h: (b, 0, 2 * pairs + h)),
                  pl.BlockSpec((1, win, 2 * GRID_W, win * GRID_W), lambda b, h: (h, 0, 0, 0))],
        out_specs=pl.BlockSpec((1, S, LANES), lambda b, h: (b, 0, h)),
        out_shape=jax.ShapeDtypeStruct((B, S, D), F32),
        scratch_shapes=[pltpu.VMEM((S, LANES), BF16), pltpu.VMEM((S, LANES), BF16)],
        compiler_params=_params(("arbitrary", "arbitrary")),
        name="neighbourhood_attention",
    )(p, p, p, bias)


def _merge_kernel(x_ref, zc_ref, yr_ref, yn_ref, gate_ref, wc_ref, bc_ref, wr_ref, wn_ref, wo_ref, o_ref):
    D = x_ref.shape[-1]
    yc = jnp.dot(zc_ref[...].astype(BF16), wc_ref[...], preferred_element_type=F32) + bc_ref[...]
    yr = jnp.dot(yr_ref[...].astype(BF16), wr_ref[...], preferred_element_type=F32)
    yn = jnp.dot(yn_ref[...].astype(BF16), wn_ref[...], preferred_element_type=F32)
    merged = gate_ref[:, :D] * yc + gate_ref[:, D:2 * D] * yr + gate_ref[:, 2 * D:] * yn
    o_ref[...] = x_ref[...] + jnp.dot(merged.astype(BF16), wo_ref[...], preferred_element_type=F32)


def _merge(x2, zc, yr, yn, gate, wc, bc, wr, wn, wo, *, tm):
    M, D = x2.shape
    row = pl.BlockSpec((tm, D), lambda i: (i, 0))
    wsp = pl.BlockSpec((D, D), lambda i: (0, 0))
    return pl.pallas_call(
        _merge_kernel,
        grid=(M // tm,),
        in_specs=[row, row, row, row, pl.BlockSpec((tm, 3 * D), lambda i: (i, 0)),
                  wsp, pl.BlockSpec((1, D), lambda i: (0, 0)), wsp, wsp, wsp],
        out_specs=row,
        out_shape=jax.ShapeDtypeStruct((M, D), F32),
        compiler_params=_params(("arbitrary",)),
        name="gated_merge",
    )(x2, zc, yr, yn, gate, wc, bc, wr, wn, wo)


def _xattn_kernel(x_ref, g_ref, wq_ref, kv_ref, wo_ref, o_ref):
    x = x_ref[0]
    D = x.shape[-1]
    dh = D // XA_HEADS
    h = _rms(x, g_ref[...]).astype(BF16)
    q = jnp.dot(h, wq_ref[...], preferred_element_type=F32)
    outs = []
    for hd in range(XA_HEADS):
        qh = q[:, hd * dh:(hd + 1) * dh].astype(BF16)
        km = kv_ref[0, :, hd * dh:(hd + 1) * dh]
        vm = kv_ref[0, :, D + hd * dh:D + (hd + 1) * dh]
        s = lax.dot_general(qh, km, (((1,), (1,)), ((), ())), preferred_element_type=F32) * dh ** -0.5
        m = jnp.max(s, axis=-1, keepdims=True)
        p = jnp.exp(s - m)
        l = jnp.sum(p, axis=-1, keepdims=True)
        outs.append(jnp.dot(p.astype(BF16), vm, preferred_element_type=F32) / l)
    o = jnp.concatenate(outs, axis=-1).astype(BF16)
    o_ref[0] = x + jnp.dot(o, wo_ref[...], preferred_element_type=F32)


def _cross_attention(x, g, wq, kv, wo, *, tm):
    B, S, D = x.shape
    Mm = kv.shape[1]
    row = pl.BlockSpec((1, tm, D), lambda b, i: (b, i, 0))
    wsp = pl.BlockSpec((D, D), lambda b, i: (0, 0))
    return pl.pallas_call(
        _xattn_kernel,
        grid=(B, S // tm),
        in_specs=[row, pl.BlockSpec((1, D), lambda b, i: (0, 0)), wsp,
                  pl.BlockSpec((1, Mm, 2 * D), lambda b, i: (b, 0, 0)), wsp],
        out_specs=row,
        out_shape=jax.ShapeDtypeStruct((B, S, D), F32),
        compiler_params=_params(("arbitrary", "arbitrary")),
        name="memory_cross_attention",
    )(x, g, wq, kv, wo)


def _mlp_kernel(x_ref, g_ref, w1_ref, w2_ref, gf_ref, o_ref, xn_ref, acc_ref, *, final_norm):
    f = pl.program_id(1)

    @pl.when(f == 0)
    def _():
        xn_ref[...] = _rms(x_ref[...], g_ref[...]).astype(BF16)
        acc_ref[...] = jnp.zeros(acc_ref.shape, F32)

    h = jnp.maximum(jnp.dot(xn_ref[...], w1_ref[...], preferred_element_type=F32), 0.0)
    acc_ref[...] += jnp.dot((h * h).astype(BF16), w2_ref[...], preferred_element_type=F32)

    @pl.when(f == pl.num_programs(1) - 1)
    def _():
        y = x_ref[...] + acc_ref[...]
        if final_norm:
            y = _rms(y, gf_ref[...])
        o_ref[...] = y


def _mlp(x2, g, w1, w2, gf, *, tm, tf, final_norm):
    M, D = x2.shape
    F = w1.shape[1]
    row = pl.BlockSpec((tm, D), lambda i, f: (i, 0))
    vec = pl.BlockSpec((1, D), lambda i, f: (0, 0))
    return pl.pallas_call(
        functools.partial(_mlp_kernel, final_norm=final_norm),
        grid=(M // tm, F // tf),
        in_specs=[row, vec, pl.BlockSpec((D, tf), lambda i, f: (0, f)),
                  pl.BlockSpec((tf, D), lambda i, f: (f, 0)), vec],
        out_specs=row,
        out_shape=jax.ShapeDtypeStruct((M, D), F32),
        scratch_shapes=[pltpu.VMEM((tm, D), BF16), pltpu.VMEM((tm, D), F32)],
        compiler_params=_params(("arbitrary", "arbitrary")),
        name="relu2_mlp",
    )(x2, g, w1, w2, gf)


def _tiles(S):
    return dict(
        proj_tm=1024,
        conv_ts=min(512, S),
        nat_tb=min(128, S),
        op_tb=16,
        scan_tc=16,
        merge_tm=256,
        xa_tm=min(512, S),
        mlp_tm=512,
        mlp_tf=1024,
    )


def _pick_tn(n, cap=1280):
    best = LANES
    for t in range(LANES, cap + 1, LANES):
        if n % t == 0:
            best = t
    return best


def kernel(x, mem, norm_mix_g, w_in, gate_b, conv_b_glu, conv_dw_w, conv_dw_b, conv_ln_g, conv_ln_b,
           conv_proj_w, conv_proj_b, rwkv_mu_prev, rwkv_mu_next, rwkv_w0, rwkv_w2, rwkv_a0, rwkv_a2,
           rwkv_g2, rwkv_k_k, rwkv_k_a, rwkv_r_k, rwkv_ln_g, rwkv_ln_b, rwkv_proj_w, na_rpb, na_proj_w,
           w_out, norm_xa_g, norm_mem_g, xa_wq, xa_wkv, xa_wo, norm_mlp_g, mlp_w1, mlp_w2, norm_f_g):
    B, S, D = x.shape
    depth = w_in.shape[0]
    Mm = mem.shape[1]
    T = _tiles(S)
    n_conv = 2 * D
    n_rwkv = 3 * D + 2 * R_DECAY + 2 * R_ICL + R_GATE
    n_na = 3 * D
    off_rwkv = n_conv
    off_na = off_rwkv + n_rwkv
    off_gate = off_na + n_na

    x2 = x.reshape(B * S, D)
    mem2 = mem.reshape(B * Mm, D)
    for l in range(depth):
        w_in_bf = w_in[l].astype(BF16)
        g_mix = norm_mix_g[l][None]

        def proj(c0, c1, name, bias=None, act=None):
            n = c1 - c0
            b = jnp.zeros((1, n), F32) if bias is None else bias[None]
            return _norm_matmul(x2, g_mix, w_in_bf[:, c0:c1], b, tm=T["proj_tm"], tn=_pick_tn(n), act=act,
                                name=name)

        p_conv = proj(0, off_rwkv, "proj_conv").reshape(B, S, n_conv)
        p_rwkv = proj(off_rwkv, off_na, "proj_rwkv").reshape(B, S, n_rwkv)
        p_na = proj(off_na, off_gate, "proj_na").reshape(B, S, n_na)
        gates = proj(off_gate, off_gate + 3 * D, "proj_gate", bias=gate_b[l], act="sigmoid")

        zc = _conformer_conv(p_conv, conv_b_glu[l], conv_dw_w[l], conv_dw_b[l], conv_ln_g[l], conv_ln_b[l],
                             ts=T["conv_ts"])

        r, k, v, d0, d1, s0, s1, g, bonus = _rwkv_nat(
            p_rwkv, rwkv_mu_prev[l], rwkv_mu_next[l], rwkv_w0[l], rwkv_w2[l], rwkv_a0[l], rwkv_a2[l],
            rwkv_g2[l], rwkv_k_a[l], rwkv_r_k[l], Tb=T["nat_tb"])
        ops = _rwkv_T(r, k, v, d0, d1, s0, s1, _pair_table(rwkv_k_k[l]), _pair_table(rwkv_k_a[l]),
                      Tb=T["op_tb"])
        yT = _rwkv_scan(*ops, Tc=T["scan_tc"])
        y_rwkv = _rwkv_post(yT, bonus, g, _pair_table(rwkv_ln_g[l]), _pair_table(rwkv_ln_b[l]), Tb=T["op_tb"])

        y_na = _neighbourhood_attention(p_na, na_rpb[l])

        x2 = _merge(x2, zc.reshape(B * S, D), y_rwkv.reshape(B * S, D), y_na.reshape(B * S, D), gates,
                    conv_proj_w[l].astype(BF16), conv_proj_b[l][None], rwkv_proj_w[l].astype(BF16),
                    na_proj_w[l].astype(BF16), w_out[l].astype(BF16), tm=T["merge_tm"])

        kv = _norm_matmul(mem2, norm_mem_g[l][None], xa_wkv[l].astype(BF16), jnp.zeros((1, 2 * D), F32),
                          tm=min(512, B * Mm), tn=1024, out_dtype=BF16, name="proj_mem_kv")
        xa = _cross_attention(x2.reshape(B, S, D), norm_xa_g[l][None], xa_wq[l].astype(BF16),
                              kv.reshape(B, Mm, 2 * D), xa_wo[l].astype(BF16), tm=T["xa_tm"])

        x2 = _mlp(xa.reshape(B * S, D), norm_mlp_g[l][None], mlp_w1[l].astype(BF16), mlp_w2[l].astype(BF16),
                  norm_f_g[None], tm=T["mlp_tm"], tf=T["mlp_tf"], final_norm=(l == depth - 1))
    return x2.reshape(B, S, D)
```

```python
import functools
import math

import jax
import jax.numpy as jnp
from jax import lax
from jax.experimental import pallas as pl
from jax.experimental.pallas import tpu as pltpu

F32 = jnp.float32
BF16 = jnp.bfloat16

NORM_EPS = 1e-6
LN_EPS = 1e-5
GRID_W = 64
CONV_W = 31
RWKV_HEAD = 64
R_DECAY = 64
R_ICL = 64
R_GATE = 128
RWKV_GN_EPS = 1e-5 * RWKV_HEAD
NA_HEAD = 64
NA_WIN_R_MAX = 8
NA_WIN_C = 16
XA_HEADS = 4
LANES = 128
VMEM_LIMIT = 48 * 1024 * 1024
EXP_NEG_HALF = math.exp(-0.5)


def _sig(x):
    return 1.0 / (1.0 + jnp.exp(-x))


def _rms(x, g):
    ms = jnp.mean(x * x, axis=-1, keepdims=True)
    return x * lax.rsqrt(ms + NORM_EPS) * g


def _params(sem):
    return pltpu.CompilerParams(dimension_semantics=sem, vmem_limit_bytes=VMEM_LIMIT)


def _norm_matmul_kernel(x_ref, g_ref, w_ref, b_ref, o_ref, xn_ref, *, act):
    @pl.when(pl.program_id(1) == 0)
    def _():
        xn_ref[...] = _rms(x_ref[...], g_ref[...]).astype(BF16)

    y = jnp.dot(xn_ref[...], w_ref[...], preferred_element_type=F32) + b_ref[...]
    if act == "sigmoid":
        y = _sig(y)
    o_ref[...] = y.astype(o_ref.dtype)


def _norm_matmul(x2, g, w_bf, bias, *, tm, tn, act=None, out_dtype=F32, name):
    M, K = x2.shape
    N = w_bf.shape[1]
    return pl.pallas_call(
        functools.partial(_norm_matmul_kernel, act=act),
        grid=(M // tm, N // tn),
        in_specs=[pl.BlockSpec((tm, K), lambda i, j: (i, 0)),
                  pl.BlockSpec((1, K), lambda i, j: (0, 0)),
                  pl.BlockSpec((K, tn), lambda i, j: (0, j)),
                  pl.BlockSpec((1, tn), lambda i, j: (0, j))],
        out_specs=pl.BlockSpec((tm, tn), lambda i, j: (i, j)),
        out_shape=jax.ShapeDtypeStruct((M, N), out_dtype),
        scratch_shapes=[pltpu.VMEM((tm, K), BF16)],
        compiler_params=_params(("arbitrary", "arbitrary")),
        name=name,
    )(x2, g, w_bf, bias)


CONV_HALO = 16
CONV_ROWS = 32
CONV_LANES = 256


def _conv_kernel(cur_ref, prev_ref, next_ref, bglu_ref, dww_ref, dwb_ref, lng_ref, lnb_ref, o_ref, zp_ref, cz_ref,
                 *, ts, C):
    i = pl.program_id(1)
    n = pl.num_programs(1)
    H = CONV_HALO
    bg = bglu_ref[...]

    def glu(u):
        return (u[:, :C] + bg[:, :C]) * _sig(u[:, C:] + bg[:, C:])

    zp_ref[H:H + ts, :] = glu(cur_ref[0])
    zp_ref[0:H, :] = jnp.where(i > 0, glu(prev_ref[0]), 0.0)
    zp_ref[H + ts:H + ts + H, :] = jnp.where(i < n - 1, glu(next_ref[0]), 0.0)

    R, LC = CONV_ROWS, CONV_LANES
    first = H - CONV_W // 2

    def conv_body(rc, carry):
        r0 = pl.multiple_of(rc * R, R)
        for lc in range(C // LC):
            ls = slice(lc * LC, (lc + 1) * LC)
            win = zp_ref[pl.ds(r0, R + 2 * H), ls]
            acc = jnp.zeros((R, LC), F32) + dwb_ref[:, ls]
            for rho in range(8):
                part = None
                for k in range(CONV_W):
                    if (first + k) % 8 != rho:
                        continue
                    term = win[first + k:first + k + R, :] * dww_ref[k:k + 1, ls]
                    part = term if part is None else part + term
                if part is not None:
                    acc = acc + part
            cz_ref[pl.ds(r0, R), ls] = acc
        return carry

    lax.fori_loop(0, ts // R, conv_body, 0)

    def ln_body(rc, carry):
        r0 = pl.multiple_of(rc * R, R)
        z = cz_ref[pl.ds(r0, R), :]
        mu = jnp.mean(z, axis=-1, keepdims=True)
        d = z - mu
        var = jnp.mean(d * d, axis=-1, keepdims=True)
        y = d * lax.rsqrt(var + LN_EPS) * lng_ref[...] + lnb_ref[...]
        o_ref[0, pl.ds(r0, R), :] = (y * _sig(y)).astype(o_ref.dtype)
        return carry

    lax.fori_loop(0, ts // R, ln_body, 0)


def _conformer_conv(u, b_glu, dw_w, dw_b, ln_g, ln_b, *, ts):
    B, S, C2 = u.shape
    C = C2 // 2
    H = CONV_HALO
    nh = ts // H
    last = S // H - 1
    dww = jnp.zeros((32, C), F32).at[:CONV_W].set(dw_w)
    return pl.pallas_call(
        functools.partial(_conv_kernel, ts=ts, C=C),
        grid=(B, S // ts),
        in_specs=[pl.BlockSpec((1, ts, C2), lambda b, i: (b, i, 0)),
                  pl.BlockSpec((1, H, C2), lambda b, i: (b, jnp.maximum(i * nh - 1, 0), 0)),
                  pl.BlockSpec((1, H, C2), lambda b, i: (b, jnp.minimum((i + 1) * nh, last), 0)),
                  pl.BlockSpec((1, C2), lambda b, i: (0, 0)),
                  pl.BlockSpec((32, C), lambda b, i: (0, 0)),
                  pl.BlockSpec((1, C), lambda b, i: (0, 0)),
                  pl.BlockSpec((1, C), lambda b, i: (0, 0)),
                  pl.BlockSpec((1, C), lambda b, i: (0, 0))],
        out_specs=pl.BlockSpec((1, ts, C), lambda b, i: (b, i, 0)),
        out_shape=jax.ShapeDtypeStruct((B, S, C), BF16),
        scratch_shapes=[pltpu.VMEM((ts + 2 * H, C), F32), pltpu.VMEM((ts, C), F32)],
        compiler_params=_params(("arbitrary", "arbitrary")),
        name="conformer_conv",
    )(u, u, u, b_glu[None], dww, dw_b[None], ln_g[None], ln_b[None])


def _rwkv_nat_kernel(cur_ref, prev_ref, next_ref, mup_ref, mun_ref, w0_ref, w2_ref, a0_ref, a2_ref, g2_ref,
                     ka_ref, rk_ref, bd_ref,
                     r_o, k_o, v_o, d0_o, d1_o, s0_o, s1_o, g_o, bon_o, *, Tb, D):
    i = pl.program_id(1)
    n = pl.num_programs(1)
    rows = lax.broadcasted_iota(jnp.int32, (Tb, LANES), 0)
    first_row = rows == 0
    last_row = rows == Tb - 1
    has_prev = i > 0
    has_next = i < n - 1

    def shifted(c0, c1):
        outs = []
        for c in range(c0, c1, LANES):
            cs = slice(c, c + LANES)
            x = cur_ref[0, :, cs]
            prow = jnp.where(has_prev, prev_ref[0, 7:8, cs], 0.0)
            nrow = jnp.where(has_next, next_ref[0, 0:1, cs], 0.0)
            xp = jnp.where(first_row, prow, pltpu.roll(x, 1, 0))
            xn = jnp.where(last_row, nrow, pltpu.roll(x, Tb - 1, 0))
            outs.append(x + mup_ref[:, cs] * (xp - x) + mun_ref[:, cs] * (xn - x))
        return outs[0] if len(outs) == 1 else jnp.concatenate(outs, axis=-1)

    r = shifted(0, D)
    k = shifted(D, 2 * D)
    v = shifted(2 * D, 3 * D)
    wd = shifted(3 * D, 3 * D + 2 * R_DECAY)
    ad = shifted(3 * D + 2 * R_DECAY, 3 * D + 2 * R_DECAY + 2 * R_ICL)
    gd = shifted(3 * D + 2 * R_DECAY + 2 * R_ICL, 3 * D + 2 * R_DECAY + 2 * R_ICL + R_GATE)

    zw = w0_ref[...] + jnp.dot(jnp.tanh(wd).astype(BF16), w2_ref[...], preferred_element_type=F32)
    dec = jnp.exp(-EXP_NEG_HALF * _sig(zw))
    asig = _sig(a0_ref[...] + jnp.dot(ad.astype(BF16), a2_ref[...], preferred_element_type=F32))
    g = jnp.dot(_sig(gd).astype(BF16), g2_ref[...], preferred_element_type=F32)

    s0 = asig[:, :D]
    s1 = asig[:, D:]
    e = r * k * (2.0 + (s0 + s1 - 2.0) * ka_ref[...]) * rk_ref[...]
    parts = []
    for c in range(0, D, LANES):
        et = e[:, c:c + LANES]
        hi = et.astype(BF16)
        lo = (et - hi.astype(F32)).astype(BF16)
        parts.append(jnp.dot(hi, bd_ref[...], preferred_element_type=F32)
                     + jnp.dot(lo, bd_ref[...], preferred_element_type=F32))
    rk_b = jnp.concatenate(parts, axis=-1)

    def store_tiles(o_ref, val):
        for q in range(D // LANES):
            o_ref[q, 0] = val[:, q * LANES:(q + 1) * LANES]

    store_tiles(r_o, r)
    store_tiles(k_o, k)
    store_tiles(v_o, v)
    store_tiles(d0_o, dec[:, :D])
    store_tiles(d1_o, dec[:, D:])
    store_tiles(s0_o, s0)
    store_tiles(s1_o, s1)
    store_tiles(g_o, g)
    store_tiles(bon_o, rk_b * v)


def _rwkv_nat(p, mu_prev, mu_next, w0, w2, a0, a2, g2, k_a, r_k, *, Tb):
    B, S, NR = p.shape
    D = (NR - 2 * R_DECAY - 2 * R_ICL - R_GATE) // 3
    nh = Tb // 8
    last = S // 8 - 1
    zero = jnp.zeros((R_DECAY, D), F32)
    w2bd = jnp.concatenate([jnp.concatenate([w2[0], zero], 1), jnp.concatenate([zero, w2[1]], 1)], 0).astype(BF16)
    a2bd = jnp.concatenate([jnp.concatenate([a2[0], zero], 1), jnp.concatenate([zero, a2[1]], 1)], 0).astype(BF16)
    half = jnp.arange(LANES) // RWKV_HEAD
    bd = (half[:, None] == half[None, :]).astype(BF16)
    full = lambda shape: pl.BlockSpec(shape, lambda b, i: (0,) * len(shape))
    out_spec = pl.BlockSpec((D // LANES, 1, Tb, LANES), lambda b, i: (0, b, i, 0))
    out = jax.ShapeDtypeStruct((D // LANES, B, S, LANES), F32)
    return pl.pallas_call(
        functools.partial(_rwkv_nat_kernel, Tb=Tb, D=D),
        grid=(B, S // Tb),
        in_specs=[pl.BlockSpec((1, Tb, NR), lambda b, i: (b, i, 0)),
                  pl.BlockSpec((1, 8, NR), lambda b, i: (b, jnp.maximum(i * nh - 1, 0), 0)),
                  pl.BlockSpec((1, 8, NR), lambda b, i: (b, jnp.minimum((i + 1) * nh, last), 0)),
                  full((1, NR)), full((1, NR)),
                  full((1, 2 * D)), full((2 * R_DECAY, 2 * D)),
                  full((1, 2 * D)), full((2 * R_ICL, 2 * D)),
                  full((R_GATE, D)), full((1, D)), full((1, D)), full((LANES, LANES))],
        out_specs=[out_spec] * 9,
        out_shape=[out] * 9,
        compiler_params=_params(("arbitrary", "arbitrary")),
        name="rwkv_tokens",
    )(p, p, p, mu_prev[None], mu_next[None], w0.reshape(1, 2 * D), w2bd, a0.reshape(1, 2 * D), a2bd,
      g2.astype(BF16), k_a[None], r_k.reshape(1, D), bd)


SCAN_PLANES = 32
SCAN_ACC = 2
SCAN_GROUP = 4
OP_R, OP_W, OP_K, OP_V, OP_A, OP_B = range(6)


def _scan_kernel(rf, rm, kf, km, vf, vm, df, dm, sf, sm_, kk_ref, ka_ref, y_ref, P_ref, ops_ref, in_ref, *, Tc):
    @pl.when(pl.program_id(0) == 0)
    def _():
        P_ref[...] = jnp.zeros(P_ref.shape, F32)

    HD = RWKV_HEAD
    half = LANES // 2
    zeros = jnp.zeros((HD, LANES), F32)
    kkT = kk_ref[...]
    kaT = ka_ref[...]
    srcs = (rf, rm, kf, km, vf, vm, df, dm, sf, sm_)
    RF, RM, KF, KM, VF, VM, DF, DM, SF, SM = range(len(srcs))
    for qi, src in enumerate(srcs):
        for n in range(half):
            in_ref[qi, n * Tc:(n + 1) * Tc, :] = src[n // src.shape[1], n % src.shape[1]]

    def stacked(f, m, s):
        top = in_ref[f, pl.ds(s, half, stride=Tc), :]
        bot = in_ref[m, pl.ds(Tc - 1 - s, half, stride=Tc), :]
        return jnp.concatenate([top, bot], axis=0).T

    def build(s, slot):
        kT = stacked(KF, KM, s)
        sT = stacked(SF, SM, s)
        kk = (kT * kkT).reshape(2, HD, LANES)
        nrm = jnp.sqrt(jnp.sum(kk * kk, axis=1, keepdims=True))
        kk = (kk / jnp.maximum(nrm, 1e-12)).reshape(LANES, LANES)
        ops_ref[slot, OP_R] = stacked(RF, RM, s)
        ops_ref[slot, OP_W] = stacked(DF, DM, s)
        ops_ref[slot, OP_K] = kT * (1.0 + (sT - 1.0) * kaT)
        ops_ref[slot, OP_V] = stacked(VF, VM, s)
        ops_ref[slot, OP_A] = -kk
        ops_ref[slot, OP_B] = kk * sT

    def sweep(s, slot):
        def op_row(idx, row):
            return ops_ref[slot, idx, pl.ds(row, 1), :]

        for g in range(LANES // HD):
            def sa_body(jb, sas):
                sas = list(sas)
                for jj in range(SCAN_PLANES):
                    row = g * HD + jb * SCAN_PLANES + jj
                    sas[jj % SCAN_ACC] = sas[jj % SCAN_ACC] + P_ref[row] * op_row(OP_A, row)
                return tuple(sas)

            sa = sum(lax.fori_loop(0, HD // SCAN_PLANES, sa_body, (zeros,) * SCAN_ACC))
            v = ops_ref[slot, OP_V, g * HD:(g + 1) * HD, :]

            def upd_body(jb, ys):
                ys = list(ys)
                for jj in range(SCAN_PLANES):
                    row = g * HD + jb * SCAN_PLANES + jj
                    pn = (P_ref[row] * op_row(OP_W, row) + sa * op_row(OP_B, row)) + v * op_row(OP_K, row)
                    P_ref[row] = pn
                    ys[jj % SCAN_ACC] = ys[jj % SCAN_ACC] + pn * op_row(OP_R, row)
                return tuple(ys)

            y_ref[s, g * HD:(g + 1) * HD, :] = sum(lax.fori_loop(0, HD // SCAN_PLANES, upd_body, (zeros,) * SCAN_ACC))

    def step_group(sg, carry):
        s0 = sg * SCAN_GROUP
        for u in range(SCAN_GROUP):
            build(s0 + u, u)
        for u in range(SCAN_GROUP):
            sweep(s0 + u, u)
        return carry

    lax.fori_loop(0, Tc // SCAN_GROUP, step_group, 0)


def _rwkv_scan(r, k, v, d0, d1, s0, s1, k_kT, k_aT, *, Tc):
    P, B, S, _ = r.shape
    nC = S // Tc
    fwd = pl.BlockSpec((P, B, Tc, LANES), lambda c: (0, 0, c, 0))
    mir = pl.BlockSpec((P, B, Tc, LANES), lambda c: (0, 0, nC - 1 - c, 0))
    tab = pl.BlockSpec((LANES, LANES), lambda c: (0, 0))
    return pl.pallas_call(
        functools.partial(_scan_kernel, Tc=Tc),
        grid=(nC,),
        in_specs=[fwd, mir, fwd, mir, fwd, mir, fwd, mir, fwd, mir, tab, tab],
        out_specs=pl.BlockSpec((Tc, LANES, LANES), lambda c: (c, 0, 0)),
        out_shape=jax.ShapeDtypeStruct((S, LANES, LANES), F32),
        scratch_shapes=[pltpu.VMEM((LANES, RWKV_HEAD, LANES), F32), pltpu.VMEM((SCAN_GROUP, 6, LANES, LANES), F32),
                        pltpu.VMEM((10, P * B * Tc, LANES), F32)],
        compiler_params=_params(("arbitrary",)),
        name="rwkv_scan",
    )(r, r, k, k, v, v, d0, d1, s0, s1, k_kT, k_aT)


def _rwkv_post_kernel(yf_ref, ym_ref, bon_ref, g_ref, lng_ref, lnb_ref, o_ref, tok_ref, *, Tb):
    half = LANES // 2

    def body(s, carry):
        y = yf_ref[s] + pltpu.roll(ym_ref[Tb - 1 - s], half, 1)
        y = y.reshape(2, RWKV_HEAD, LANES)
        mu = jnp.mean(y, axis=1, keepdims=True)
        d = y - mu
        var = jnp.mean(d * d, axis=1, keepdims=True)
        yn = (d * lax.rsqrt(var + RWKV_GN_EPS)).reshape(LANES, LANES) * lng_ref[...] + lnb_ref[...]
        tok_ref[pl.ds(s, half, stride=Tb), :] = yn.T[:half]
        return carry

    lax.fori_loop(0, Tb, body, 0, unroll=2)
    nb = o_ref.shape[1]
    for n in range(half):
        p, b = n // nb, n % nb
        o_ref[p, b] = (tok_ref[n * Tb:(n + 1) * Tb, :] + bon_ref[p, b]) * g_ref[p, b]


def _rwkv_post(yT, bonus, g, ln_gT, ln_bT, *, Tb):
    P, B, S, _ = bonus.shape
    nC = S // Tb
    tok = pl.BlockSpec((P, B, Tb, LANES), lambda c: (0, 0, c, 0))
    tab = pl.BlockSpec((LANES, LANES), lambda c: (0, 0))
    return pl.pallas_call(
        functools.partial(_rwkv_post_kernel, Tb=Tb),
        grid=(nC,),
        in_specs=[pl.BlockSpec((Tb, LANES, LANES), lambda c: (c, 0, 0)),
                  pl.BlockSpec((Tb, LANES, LANES), lambda c: (nC - 1 - c, 0, 0)),
                  tok, tok, tab, tab],
        out_specs=tok,
        out_shape=jax.ShapeDtypeStruct((P, B, S, LANES), F32),
        scratch_shapes=[pltpu.VMEM((P * B * Tb, LANES), F32)],
        compiler_params=_params(("arbitrary",)),
        name="rwkv_groupnorm",
    )(yT, yT, bonus, g, ln_gT, ln_bT)


def _pair_table(p, batch):
    pairs = p.shape[0] // LANES
    t = jnp.repeat(p.reshape(pairs, LANES).T, batch, axis=1)
    return jnp.tile(t, (1, LANES // (pairs * batch)))


NA_GROUP = 4


def _na_kernel(q_ref, k_ref, v_ref, bias_ref, o_ref, *, rows, win):
    GW = GRID_W
    lane = lax.broadcasted_iota(jnp.int32, (2 * GW, LANES), 1)
    head = lax.broadcasted_iota(jnp.int32, (2 * GW, LANES), 0) // GW
    own = (lane // NA_HEAD) == head
    qc = lax.broadcasted_iota(jnp.int32, (2 * GW, win * GW), 0) % GW
    kc = lax.broadcasted_iota(jnp.int32, (2 * GW, win * GW), 1) % GW
    cs = jnp.clip(qc - NA_WIN_C // 2, 0, GW - NA_WIN_C)
    ok = (kc >= cs) & (kc < cs + NA_WIN_C)
    low = lax.broadcasted_iota(jnp.int32, (GW, LANES), 1) < NA_HEAD
    scale = NA_HEAD ** -0.5

    def body(rg, carry):
        scores, vbs, qss = [], [], []
        for i in range(NA_GROUP):
            r = rg * NA_GROUP + i
            r0 = jnp.clip(r - win // 2, 0, rows - win)
            qs = pl.multiple_of(r * GW, GW)
            ks = pl.multiple_of(r0 * GW, GW)
            q = q_ref[0, pl.ds(qs, GW), :] * scale
            qq = jnp.where(own, jnp.concatenate([q, q], axis=0), 0.0).astype(BF16)
            kb = k_ref[0, pl.ds(ks, win * GW), :]
            s = lax.dot_general(qq, kb, (((1,), (1,)), ((), ())), preferred_element_type=F32)
            scores.append(jnp.where(ok, s + bias_ref[0, r - r0], -1e30))
            vbs.append(v_ref[0, pl.ds(ks, win * GW), :])
            qss.append(qs)
        probs, sums = [], []
        for s in scores:
            p = jnp.exp(s - jnp.max(s, axis=-1, keepdims=True))
            sums.append(jnp.sum(p, axis=-1, keepdims=True))
            probs.append(p.astype(BF16))
        for p, l, vb, qs in zip(probs, sums, vbs, qss):
            o = jnp.dot(p, vb, preferred_element_type=F32) / l
            o_ref[0, pl.ds(qs, GW), :] = jnp.where(low, o[:GW], o[GW:]).astype(o_ref.dtype)
        return carry

    lax.fori_loop(0, rows // NA_GROUP, body, 0)


def _na_bias(rpb, win):
    col = jnp.arange(GRID_W)
    dx = jnp.clip(col[None, :] - col[:, None], 1 - NA_WIN_C, NA_WIN_C - 1) + NA_WIN_C - 1
    d = jnp.arange(win)
    p = jnp.arange(win)
    dy = p[None, :] - d[:, None] + NA_WIN_R_MAX - 1
    t = rpb[:, dy][:, :, :, dx]
    H = rpb.shape[0]
    t = t.reshape(H // 2, 2, win, win, GRID_W, GRID_W).transpose(0, 2, 1, 4, 3, 5)
    return t.reshape(H // 2, win, 2 * GRID_W, win * GRID_W).astype(F32)


def _neighbourhood_attention(p, rpb):
    B, S, D3 = p.shape
    D = D3 // 3
    pairs = D // LANES
    rows = S // GRID_W
    win = min(NA_WIN_R_MAX, rows)
    bias = _na_bias(rpb, win)
    return pl.pallas_call(
        functools.partial(_na_kernel, rows=rows, win=win),
        grid=(B, pairs),
        in_specs=[pl.BlockSpec((1, S, LANES), lambda b, h: (b, 0, h)),
                  pl.BlockSpec((1, S, LANES), lambda b, h: (b, 0, pairs + h)),
                  pl.BlockSpec((1, S, LANES), lambda b, h: (b, 0, 2 * pairs + h)),
                  pl.BlockSpec((1, win, 2 * GRID_W, win * GRID_W), lambda b, h: (h, 0, 0, 0))],
        out_specs=pl.BlockSpec((1, S, LANES), lambda b, h: (b, 0, h)),
        out_shape=jax.ShapeDtypeStruct((B, S, D), BF16),
        compiler_params=_params(("arbitrary", "arbitrary")),
        name="neighbourhood_attention",
    )(p, p, p, bias)


def _merge_kernel(x_ref, zc_ref, yr_ref, yn_ref, gate_ref, wc_ref, bc_ref, wr_ref, wn_ref, wo_ref, o_ref):
    D = x_ref.shape[-1]
    yc = jnp.dot(zc_ref[...].astype(BF16), wc_ref[...], preferred_element_type=F32) + bc_ref[...]
    yr_in = jnp.concatenate([yr_ref[q] for q in range(yr_ref.shape[0])], axis=-1)
    yr = jnp.dot(yr_in.astype(BF16), wr_ref[...], preferred_element_type=F32)
    yn = jnp.dot(yn_ref[...].astype(BF16), wn_ref[...], preferred_element_type=F32)
    merged = gate_ref[:, :D] * yc + gate_ref[:, D:2 * D] * yr + gate_ref[:, 2 * D:] * yn
    o_ref[...] = x_ref[...] + jnp.dot(merged.astype(BF16), wo_ref[...], preferred_element_type=F32)


def _merge(x2, zc, yr, yn, gate, wc, bc, wr, wn, wo, *, tm):
    M, D = x2.shape
    row = pl.BlockSpec((tm, D), lambda i: (i, 0))
    wsp = pl.BlockSpec((D, D), lambda i: (0, 0))
    return pl.pallas_call(
        _merge_kernel,
        grid=(M // tm,),
        in_specs=[row, row, pl.BlockSpec((D // LANES, tm, LANES), lambda i: (0, i, 0)), row,
                  pl.BlockSpec((tm, 3 * D), lambda i: (i, 0)),
                  wsp, pl.BlockSpec((1, D), lambda i: (0, 0)), wsp, wsp, wsp],
        out_specs=row,
        out_shape=jax.ShapeDtypeStruct((M, D), F32),
        compiler_params=_params(("arbitrary",)),
        name="gated_merge",
    )(x2, zc, yr, yn, gate, wc, bc, wr, wn, wo)


def _xattn_kernel(x_ref, g_ref, wq_ref, kv_ref, wo_ref, o_ref):
    x = x_ref[0]
    D = x.shape[-1]
    dh = D // XA_HEADS
    h = _rms(x, g_ref[...]).astype(BF16)
    q = jnp.dot(h, wq_ref[...], preferred_element_type=F32)
    outs = []
    for hd in range(XA_HEADS):
        qh = q[:, hd * dh:(hd + 1) * dh].astype(BF16)
        km = kv_ref[0, :, hd * dh:(hd + 1) * dh]
        vm = kv_ref[0, :, D + hd * dh:D + (hd + 1) * dh]
        s = lax.dot_general(qh, km, (((1,), (1,)), ((), ())), preferred_element_type=F32) * dh ** -0.5
        m = jnp.max(s, axis=-1, keepdims=True)
        p = jnp.exp(s - m)
        l = jnp.sum(p, axis=-1, keepdims=True)
        outs.append(jnp.dot(p.astype(BF16), vm, preferred_element_type=F32) / l)
    o = jnp.concatenate(outs, axis=-1).astype(BF16)
    o_ref[0] = x + jnp.dot(o, wo_ref[...], preferred_element_type=F32)


def _cross_attention(x, g, wq, kv, wo, *, tm):
    B, S, D = x.shape
    Mm = kv.shape[1]
    row = pl.BlockSpec((1, tm, D), lambda b, i: (b, i, 0))
    wsp = pl.BlockSpec((D, D), lambda b, i: (0, 0))
    return pl.pallas_call(
        _xattn_kernel,
        grid=(B, S // tm),
        in_specs=[row, pl.BlockSpec((1, D), lambda b, i: (0, 0)), wsp,
                  pl.BlockSpec((1, Mm, 2 * D), lambda b, i: (b, 0, 0)), wsp],
        out_specs=row,
        out_shape=jax.ShapeDtypeStruct((B, S, D), F32),
        compiler_params=_params(("arbitrary", "arbitrary")),
        name="memory_cross_attention",
    )(x, g, wq, kv, wo)


def _mlp_kernel(x_ref, g_ref, w1_ref, w2_ref, gf_ref, o_ref, xn_ref, acc_ref, *, final_norm):
    f = pl.program_id(1)

    @pl.when(f == 0)
    def _():
        xn_ref[...] = _rms(x_ref[...], g_ref[...]).astype(BF16)
        acc_ref[...] = jnp.zeros(acc_ref.shape, F32)

    h = jnp.maximum(jnp.dot(xn_ref[...], w1_ref[...], preferred_element_type=F32), 0.0)
    acc_ref[...] += jnp.dot((h * h).astype(BF16), w2_ref[...], preferred_element_type=F32)

    @pl.when(f == pl.num_programs(1) - 1)
    def _():
        y = x_ref[...] + acc_ref[...]
        if final_norm:
            y = _rms(y, gf_ref[...])
        o_ref[...] = y


def _mlp(x2, g, w1, w2, gf, *, tm, tf, final_norm):
    M, D = x2.shape
    F = w1.shape[1]
    row = pl.BlockSpec((tm, D), lambda i, f: (i, 0))
    vec = pl.BlockSpec((1, D), lambda i, f: (0, 0))
    return pl.pallas_call(
        functools.partial(_mlp_kernel, final_norm=final_norm),
        grid=(M // tm, F // tf),
        in_specs=[row, vec, pl.BlockSpec((D, tf), lambda i, f: (0, f)),
                  pl.BlockSpec((tf, D), lambda i, f: (f, 0)), vec],
        out_specs=row,
        out_shape=jax.ShapeDtypeStruct((M, D), F32),
        scratch_shapes=[pltpu.VMEM((tm, D), BF16), pltpu.VMEM((tm, D), F32)],
        compiler_params=_params(("arbitrary", "arbitrary")),
        name="relu2_mlp",
    )(x2, g, w1, w2, gf)


def _tiles(S):
    return dict(
        proj_tm=1024,
        conv_ts=min(512, S),
        nat_tb=min(128, S),
        post_tb=16,
        scan_tc=16,
        merge_tm=256,
        xa_tm=min(512, S),
        mlp_tm=512,
        mlp_tf=1024,
    )


def _pick_tn(n, cap=1280):
    best = LANES
    for t in range(LANES, cap + 1, LANES):
        if n % t == 0:
            best = t
    return best


def kernel(x, mem, norm_mix_g, w_in, gate_b, conv_b_glu, conv_dw_w, conv_dw_b, conv_ln_g, conv_ln_b,
           conv_proj_w, conv_proj_b, rwkv_mu_prev, rwkv_mu_next, rwkv_w0, rwkv_w2, rwkv_a0, rwkv_a2,
           rwkv_g2, rwkv_k_k, rwkv_k_a, rwkv_r_k, rwkv_ln_g, rwkv_ln_b, rwkv_proj_w, na_rpb, na_proj_w,
           w_out, norm_xa_g, norm_mem_g, xa_wq, xa_wkv, xa_wo, norm_mlp_g, mlp_w1, mlp_w2, norm_f_g):
    B, S, D = x.shape
    depth = w_in.shape[0]
    Mm = mem.shape[1]
    T = _tiles(S)
    n_conv = 2 * D
    n_rwkv = 3 * D + 2 * R_DECAY + 2 * R_ICL + R_GATE
    n_na = 3 * D
    off_rwkv = n_conv
    off_na = off_rwkv + n_rwkv
    off_gate = off_na + n_na

    x2 = x.reshape(B * S, D)
    mem2 = mem.reshape(B * Mm, D)
    for l in range(depth):
        w_in_bf = w_in[l].astype(BF16)
        g_mix = norm_mix_g[l][None]

        def proj(c0, c1, name, bias=None, act=None, out_dtype=F32):
            n = c1 - c0
            b = jnp.zeros((1, n), F32) if bias is None else bias[None]
            return _norm_matmul(x2, g_mix, w_in_bf[:, c0:c1], b, tm=T["proj_tm"], tn=_pick_tn(n), act=act,
                                out_dtype=out_dtype, name=name)

        p_conv = proj(0, off_rwkv, "proj_conv").reshape(B, S, n_conv)
        p_rwkv = proj(off_rwkv, off_na, "proj_rwkv").reshape(B, S, n_rwkv)
        p_na = proj(off_na, off_gate, "proj_na", out_dtype=BF16).reshape(B, S, n_na)
        gates = proj(off_gate, off_gate + 3 * D, "proj_gate", bias=gate_b[l], act="sigmoid")

        zc = _conformer_conv(p_conv, conv_b_glu[l], conv_dw_w[l], conv_dw_b[l], conv_ln_g[l], conv_ln_b[l],
                             ts=T["conv_ts"])

        r, k, v, d0, d1, s0, s1, g, bonus = _rwkv_nat(
            p_rwkv, rwkv_mu_prev[l], rwkv_mu_next[l], rwkv_w0[l], rwkv_w2[l], rwkv_a0[l], rwkv_a2[l],
            rwkv_g2[l], rwkv_k_a[l], rwkv_r_k[l], Tb=T["nat_tb"])
        yT = _rwkv_scan(r, k, v, d0, d1, s0, s1, _pair_table(rwkv_k_k[l], B), _pair_table(rwkv_k_a[l], B),
                        Tc=T["scan_tc"])
        y_rwkv = _rwkv_post(yT, bonus, g, _pair_table(rwkv_ln_g[l], B), _pair_table(rwkv_ln_b[l], B),
                            Tb=T["post_tb"])

        y_na = _neighbourhood_attention(p_na, na_rpb[l])

        x2 = _merge(x2, zc.reshape(B * S, D), y_rwkv.reshape(D // LANES, B * S, LANES), y_na.reshape(B * S, D),
                    gates, conv_proj_w[l].astype(BF16), conv_proj_b[l][None], rwkv_proj_w[l].astype(BF16),
                    na_proj_w[l].astype(BF16), w_out[l].astype(BF16), tm=T["merge_tm"])

        kv = _norm_matmul(mem2, norm_mem_g[l][None], xa_wkv[l].astype(BF16), jnp.zeros((1, 2 * D), F32),
                          tm=min(512, B * Mm), tn=1024, out_dtype=BF16, name="proj_mem_kv")
        xa = _cross_attention(x2.reshape(B, S, D), norm_xa_g[l][None], xa_wq[l].astype(BF16),
                              kv.reshape(B, Mm, 2 * D), xa_wo[l].astype(BF16), tm=T["xa_tm"])

        x2 = _mlp(xa.reshape(B * S, D), norm_mlp_g[l][None], mlp_w1[l].astype(BF16), mlp_w2[l].astype(BF16),
                  norm_f_g[None], tm=T["mlp_tm"], tf=T["mlp_tf"], final_norm=(l == depth - 1))
    return x2.reshape(B, S, D)
```

```python
import functools
import math

import jax
import jax.numpy as jnp
from jax import lax
from jax.experimental import pallas as pl
from jax.experimental.pallas import tpu as pltpu

F32 = jnp.float32
BF16 = jnp.bfloat16

NORM_EPS = 1e-6
LN_EPS = 1e-5
GRID_W = 64
CONV_W = 31
RWKV_HEAD = 64
R_DECAY = 64
R_ICL = 64
R_GATE = 128
RWKV_GN_EPS = 1e-5 * RWKV_HEAD
NA_HEAD = 64
NA_WIN_R_MAX = 8
NA_WIN_C = 16
XA_HEADS = 4
LANES = 128
VMEM_LIMIT = 48 * 1024 * 1024
EXP_NEG_HALF = math.exp(-0.5)


def _sig(x):
    return 1.0 / (1.0 + jnp.exp(-x))


def _rms(x, g):
    ms = jnp.mean(x * x, axis=-1, keepdims=True)
    return x * lax.rsqrt(ms + NORM_EPS) * g


def _params(sem):
    return pltpu.CompilerParams(dimension_semantics=sem, vmem_limit_bytes=VMEM_LIMIT)


def _norm_matmul_kernel(x_ref, g_ref, w_ref, b_ref, o_ref, xn_ref):
    @pl.when(pl.program_id(1) == 0)
    def _():
        xn_ref[...] = _rms(x_ref[...], g_ref[...]).astype(BF16)

    y = jnp.dot(xn_ref[...], w_ref[...], preferred_element_type=F32) + b_ref[...]
    o_ref[...] = y.astype(o_ref.dtype)


def _norm_matmul(x2, g, w_bf, bias, *, tm, tn, out_dtype=F32, emit_xn=False, name):
    M, K = x2.shape
    N = w_bf.shape[1]
    out_specs = pl.BlockSpec((tm, tn), lambda i, j: (i, j))
    out_shape = jax.ShapeDtypeStruct((M, N), out_dtype)
    scratch = [pltpu.VMEM((tm, K), BF16)]
    if emit_xn:
        out_specs = [out_specs, pl.BlockSpec((tm, K), lambda i, j: (i, 0))]
        out_shape = [out_shape, jax.ShapeDtypeStruct((M, K), BF16)]
        scratch = []
    return pl.pallas_call(
        _norm_matmul_kernel,
        grid=(M // tm, N // tn),
        in_specs=[pl.BlockSpec((tm, K), lambda i, j: (i, 0)),
                  pl.BlockSpec((1, K), lambda i, j: (0, 0)),
                  pl.BlockSpec((K, tn), lambda i, j: (0, j)),
                  pl.BlockSpec((1, tn), lambda i, j: (0, j))],
        out_specs=out_specs,
        out_shape=out_shape,
        scratch_shapes=scratch,
        compiler_params=_params(("arbitrary", "arbitrary")),
        name=name,
    )(x2, g, w_bf, bias)


def _matmul_kernel(x_ref, w_ref, b_ref, o_ref, *, act):
    y = jnp.dot(x_ref[...], w_ref[...], preferred_element_type=F32) + b_ref[...]
    if act == "sigmoid":
        y = _sig(y)
    o_ref[...] = y.astype(o_ref.dtype)


def _matmul(xn, w_bf, bias, *, tm, tn, act=None, out_dtype=F32, name):
    M, K = xn.shape
    N = w_bf.shape[1]
    return pl.pallas_call(
        functools.partial(_matmul_kernel, act=act),
        grid=(M // tm, N // tn),
        in_specs=[pl.BlockSpec((tm, K), lambda i, j: (i, 0)),
                  pl.BlockSpec((K, tn), lambda i, j: (0, j)),
                  pl.BlockSpec((1, tn), lambda i, j: (0, j))],
        out_specs=pl.BlockSpec((tm, tn), lambda i, j: (i, j)),
        out_shape=jax.ShapeDtypeStruct((M, N), out_dtype),
        compiler_params=_params(("arbitrary", "arbitrary")),
        name=name,
    )(xn, w_bf, bias)


CONV_HALO = 16
CONV_ROWS = 128
CONV_LANES = 128


def _conv_kernel(cur_ref, prev_ref, next_ref, bglu_ref, dww_ref, dwb_ref, lng_ref, lnb_ref, o_ref, zp_ref, cz_ref,
                 *, ts, C):
    i = pl.program_id(1)
    n = pl.num_programs(1)
    H = CONV_HALO
    bg = bglu_ref[...]

    def glu(u):
        return (u[:, :C] + bg[:, :C]) * _sig(u[:, C:] + bg[:, C:])

    zp_ref[H:H + ts, :] = glu(cur_ref[0])
    zp_ref[0:H, :] = jnp.where(i > 0, glu(prev_ref[0]), 0.0)
    zp_ref[H + ts:H + ts + H, :] = jnp.where(i < n - 1, glu(next_ref[0]), 0.0)

    R, LC = CONV_ROWS, CONV_LANES
    first = H - CONV_W // 2

    def conv_body(rc, carry):
        r0 = pl.multiple_of(rc * R, R)
        for lc in range(C // LC):
            ls = slice(lc * LC, (lc + 1) * LC)
            win = zp_ref[pl.ds(r0, R + 2 * H), ls]
            acc = jnp.zeros((R, LC), F32) + dwb_ref[:, ls]
            for rho in range(8):
                part = None
                for k in range(CONV_W):
                    if (first + k) % 8 != rho:
                        continue
                    term = win[first + k:first + k + R, :] * dww_ref[k:k + 1, ls]
                    part = term if part is None else part + term
                if part is not None:
                    acc = acc + part
            cz_ref[pl.ds(r0, R), ls] = acc
        return carry

    lax.fori_loop(0, ts // R, conv_body, 0)

    def ln_body(rc, carry):
        r0 = pl.multiple_of(rc * R, R)
        z = cz_ref[pl.ds(r0, R), :]
        mu = jnp.mean(z, axis=-1, keepdims=True)
        d = z - mu
        var = jnp.mean(d * d, axis=-1, keepdims=True)
        y = d * lax.rsqrt(var + LN_EPS) * lng_ref[...] + lnb_ref[...]
        o_ref[0, pl.ds(r0, R), :] = (y * _sig(y)).astype(o_ref.dtype)
        return carry

    lax.fori_loop(0, ts // R, ln_body, 0, unroll=2)


def _conformer_conv(u, b_glu, dw_w, dw_b, ln_g, ln_b, *, ts):
    B, S, C2 = u.shape
    C = C2 // 2
    H = CONV_HALO
    nh = ts // H
    last = S // H - 1
    dww = jnp.zeros((32, C), F32).at[:CONV_W].set(dw_w)
    return pl.pallas_call(
        functools.partial(_conv_kernel, ts=ts, C=C),
        grid=(B, S // ts),
        in_specs=[pl.BlockSpec((1, ts, C2), lambda b, i: (b, i, 0)),
                  pl.BlockSpec((1, H, C2), lambda b, i: (b, jnp.maximum(i * nh - 1, 0), 0)),
                  pl.BlockSpec((1, H, C2), lambda b, i: (b, jnp.minimum((i + 1) * nh, last), 0)),
                  pl.BlockSpec((1, C2), lambda b, i: (0, 0)),
                  pl.BlockSpec((32, C), lambda b, i: (0, 0)),
                  pl.BlockSpec((1, C), lambda b, i: (0, 0)),
                  pl.BlockSpec((1, C), lambda b, i: (0, 0)),
                  pl.BlockSpec((1, C), lambda b, i: (0, 0))],
        out_specs=pl.BlockSpec((1, ts, C), lambda b, i: (b, i, 0)),
        out_shape=jax.ShapeDtypeStruct((B, S, C), BF16),
        scratch_shapes=[pltpu.VMEM((ts + 2 * H, C), F32), pltpu.VMEM((ts, C), F32)],
        compiler_params=_params(("arbitrary", "arbitrary")),
        name="conformer_conv",
    )(u, u, u, b_glu[None], dww, dw_b[None], ln_g[None], ln_b[None])


def _rwkv_nat_kernel(cur_ref, prev_ref, next_ref, mup_ref, mun_ref, w0_ref, w2_ref, a0_ref, a2_ref, g2_ref,
                     ka_ref, rk_ref, bd_ref,
                     r_o, k_o, v_o, d0_o, d1_o, s0_o, s1_o, g_o, bon_o, *, Tb, D):
    i = pl.program_id(1)
    n = pl.num_programs(1)
    rows = lax.broadcasted_iota(jnp.int32, (Tb, LANES), 0)
    first_row = rows == 0
    last_row = rows == Tb - 1
    has_prev = i > 0
    has_next = i < n - 1

    def shifted(c0, c1):
        outs = []
        for c in range(c0, c1, LANES):
            cs = slice(c, c + LANES)
            x = cur_ref[0, :, cs]
            prow = jnp.where(has_prev, prev_ref[0, 7:8, cs], 0.0)
            nrow = jnp.where(has_next, next_ref[0, 0:1, cs], 0.0)
            xp = jnp.where(first_row, prow, pltpu.roll(x, 1, 0))
            xn = jnp.where(last_row, nrow, pltpu.roll(x, Tb - 1, 0))
            outs.append(x + mup_ref[:, cs] * (xp - x) + mun_ref[:, cs] * (xn - x))
        return outs[0] if len(outs) == 1 else jnp.concatenate(outs, axis=-1)

    r = shifted(0, D)
    k = shifted(D, 2 * D)
    v = shifted(2 * D, 3 * D)
    wd = shifted(3 * D, 3 * D + 2 * R_DECAY)
    ad = shifted(3 * D + 2 * R_DECAY, 3 * D + 2 * R_DECAY + 2 * R_ICL)
    gd = shifted(3 * D + 2 * R_DECAY + 2 * R_ICL, 3 * D + 2 * R_DECAY + 2 * R_ICL + R_GATE)

    zw = w0_ref[...] + jnp.dot(jnp.tanh(wd).astype(BF16), w2_ref[...], preferred_element_type=F32)
    dec = jnp.exp(-EXP_NEG_HALF * _sig(zw))
    asig = _sig(a0_ref[...] + jnp.dot(ad.astype(BF16), a2_ref[...], preferred_element_type=F32))
    g = jnp.dot(_sig(gd).astype(BF16), g2_ref[...], preferred_element_type=F32)

    s0 = asig[:, :D]
    s1 = asig[:, D:]
    e = r * k * (2.0 + (s0 + s1 - 2.0) * ka_ref[...]) * rk_ref[...]
    parts = []
    for c in range(0, D, LANES):
        et = e[:, c:c + LANES]
        hi = et.astype(BF16)
        lo = (et - hi.astype(F32)).astype(BF16)
        parts.append(jnp.dot(hi, bd_ref[...], preferred_element_type=F32)
                     + jnp.dot(lo, bd_ref[...], preferred_element_type=F32))
    rk_b = jnp.concatenate(parts, axis=-1)

    def store_tiles(o_ref, val):
        for q in range(D // LANES):
            o_ref[q, 0] = val[:, q * LANES:(q + 1) * LANES]

    store_tiles(r_o, r)
    store_tiles(k_o, k)
    store_tiles(v_o, v)
    store_tiles(d0_o, dec[:, :D])
    store_tiles(d1_o, dec[:, D:])
    store_tiles(s0_o, s0)
    store_tiles(s1_o, s1)
    store_tiles(g_o, g)
    store_tiles(bon_o, rk_b * v)


def _rwkv_nat(p, mu_prev, mu_next, w0, w2, a0, a2, g2, k_a, r_k, *, Tb):
    B, S, NR = p.shape
    D = (NR - 2 * R_DECAY - 2 * R_ICL - R_GATE) // 3
    nh = Tb // 8
    last = S // 8 - 1
    zero = jnp.zeros((R_DECAY, D), F32)
    w2bd = jnp.concatenate([jnp.concatenate([w2[0], zero], 1), jnp.concatenate([zero, w2[1]], 1)], 0).astype(BF16)
    a2bd = jnp.concatenate([jnp.concatenate([a2[0], zero], 1), jnp.concatenate([zero, a2[1]], 1)], 0).astype(BF16)
    half = jnp.arange(LANES) // RWKV_HEAD
    bd = (half[:, None] == half[None, :]).astype(BF16)
    full = lambda shape: pl.BlockSpec(shape, lambda b, i: (0,) * len(shape))
    out_spec = pl.BlockSpec((D // LANES, 1, Tb, LANES), lambda b, i: (0, b, i, 0))
    out = jax.ShapeDtypeStruct((D // LANES, B, S, LANES), F32)
    return pl.pallas_call(
        functools.partial(_rwkv_nat_kernel, Tb=Tb, D=D),
        grid=(B, S // Tb),
        in_specs=[pl.BlockSpec((1, Tb, NR), lambda b, i: (b, i, 0)),
                  pl.BlockSpec((1, 8, NR), lambda b, i: (b, jnp.maximum(i * nh - 1, 0), 0)),
                  pl.BlockSpec((1, 8, NR), lambda b, i: (b, jnp.minimum((i + 1) * nh, last), 0)),
                  full((1, NR)), full((1, NR)),
                  full((1, 2 * D)), full((2 * R_DECAY, 2 * D)),
                  full((1, 2 * D)), full((2 * R_ICL, 2 * D)),
                  full((R_GATE, D)), full((1, D)), full((1, D)), full((LANES, LANES))],
        out_specs=[out_spec] * 9,
        out_shape=[out] * 9,
        compiler_params=_params(("arbitrary", "arbitrary")),
        name="rwkv_tokens",
    )(p, p, p, mu_prev[None], mu_next[None], w0.reshape(1, 2 * D), w2bd, a0.reshape(1, 2 * D), a2bd,
      g2.astype(BF16), k_a[None], r_k.reshape(1, D), bd)


SCAN_PLANES = 32
SCAN_ACC = 2
SCAN_GROUP = 4
OP_R, OP_K, OP_V, OP_A, OP_B = range(5)


def _scan_kernel(rf, rm, kf, km, vf, vm, df, dm, sf, sm_, kk_ref, ka_ref, y_ref, P_ref, ops_ref, in_ref, c_ref,
                 *, Tc):
    @pl.when(pl.program_id(0) == 0)
    def _():
        P_ref[...] = jnp.zeros(P_ref.shape, F32)

    c_ref[...] = jnp.ones(c_ref.shape, F32)

    HD = RWKV_HEAD
    half = LANES // 2
    zeros = jnp.zeros((HD, LANES), F32)
    kkT = kk_ref[...]
    kaT = ka_ref[...]
    srcs = (rf, rm, kf, km, vf, vm, df, dm, sf, sm_)
    RF, RM, KF, KM, VF, VM, DF, DM, SF, SM = range(len(srcs))
    for qi, src in enumerate(srcs):
        for n in range(half):
            in_ref[qi, n * Tc:(n + 1) * Tc, :] = src[n // src.shape[1], n % src.shape[1]]

    def stacked(f, m, s):
        top = in_ref[f, pl.ds(s, half, stride=Tc), :]
        bot = in_ref[m, pl.ds(Tc - 1 - s, half, stride=Tc), :]
        return jnp.concatenate([top, bot], axis=0).T

    def build(s, slot):
        kT = stacked(KF, KM, s)
        sT = stacked(SF, SM, s)
        kk = (kT * kkT).reshape(2, HD, LANES)
        nrm = jnp.sqrt(jnp.sum(kk * kk, axis=1, keepdims=True))
        kk = (kk / jnp.maximum(nrm, 1e-12)).reshape(LANES, LANES)
        c_prev = c_ref[...]
        c = c_prev * stacked(DF, DM, s)
        c_ref[...] = c
        c_inv = 1.0 / c
        ops_ref[slot, OP_R] = stacked(RF, RM, s) * c
        ops_ref[slot, OP_K] = kT * (1.0 + (sT - 1.0) * kaT) * c_inv
        ops_ref[slot, OP_V] = stacked(VF, VM, s)
        ops_ref[slot, OP_A] = -kk * c_prev
        ops_ref[slot, OP_B] = kk * sT * c_inv

    def sweep(s, slot, unscale):
        def op_row(idx, row):
            return ops_ref[slot, idx, pl.ds(row, 1), :]

        for g in range(LANES // HD):
            def sa_body(jb, sas):
                sas = list(sas)
                for jj in range(SCAN_PLANES):
                    row = g * HD + jb * SCAN_PLANES + jj
                    sas[jj % SCAN_ACC] = sas[jj % SCAN_ACC] + P_ref[row] * op_row(OP_A, row)
                return tuple(sas)

            sa = sum(lax.fori_loop(0, HD // SCAN_PLANES, sa_body, (zeros,) * SCAN_ACC))
            v = ops_ref[slot, OP_V, g * HD:(g + 1) * HD, :]

            def upd_body(jb, ys):
                ys = list(ys)
                for jj in range(SCAN_PLANES):
                    row = g * HD + jb * SCAN_PLANES + jj
                    pn = (P_ref[row] + sa * op_row(OP_B, row)) + v * op_row(OP_K, row)
                    ys[jj % SCAN_ACC] = ys[jj % SCAN_ACC] + pn * op_row(OP_R, row)
                    P_ref[row] = pn * c_ref[pl.ds(row, 1), :] if unscale else pn
                return tuple(ys)

            y_ref[s, g * HD:(g + 1) * HD, :] = sum(lax.fori_loop(0, HD // SCAN_PLANES, upd_body, (zeros,) * SCAN_ACC))

    def step_group(sg, last):
        s0 = sg * SCAN_GROUP
        for u in range(SCAN_GROUP):
            build(s0 + u, u)
        for u in range(SCAN_GROUP):
            sweep(s0 + u, u, last and u == SCAN_GROUP - 1)

    groups = Tc // SCAN_GROUP

    def group_body(sg, carry):
        step_group(sg, False)
        return carry

    lax.fori_loop(0, groups - 1, group_body, 0)
    step_group(groups - 1, True)


def _rwkv_scan(r, k, v, d0, d1, s0, s1, k_kT, k_aT, *, Tc):
    P, B, S, _ = r.shape
    nC = S // Tc
    fwd = pl.BlockSpec((P, B, Tc, LANES), lambda c: (0, 0, c, 0))
    mir = pl.BlockSpec((P, B, Tc, LANES), lambda c: (0, 0, nC - 1 - c, 0))
    tab = pl.BlockSpec((LANES, LANES), lambda c: (0, 0))
    return pl.pallas_call(
        functools.partial(_scan_kernel, Tc=Tc),
        grid=(nC,),
        in_specs=[fwd, mir, fwd, mir, fwd, mir, fwd, mir, fwd, mir, tab, tab],
        out_specs=pl.BlockSpec((Tc, LANES, LANES), lambda c: (c, 0, 0)),
        out_shape=jax.ShapeDtypeStruct((S, LANES, LANES), F32),
        scratch_shapes=[pltpu.VMEM((LANES, RWKV_HEAD, LANES), F32), pltpu.VMEM((SCAN_GROUP, 5, LANES, LANES), F32),
                        pltpu.VMEM((10, P * B * Tc, LANES), F32), pltpu.VMEM((LANES, LANES), F32)],
        compiler_params=_params(("arbitrary",)),
        name="rwkv_scan",
    )(r, r, k, k, v, v, d0, d1, s0, s1, k_kT, k_aT)


def _rwkv_post_kernel(yf_ref, ym_ref, bonf_ref, bonm_ref, gf_ref, gm_ref, lng_ref, lnb_ref, of_ref, om_ref, tok_ref,
                      *, Tb):
    half = LANES // 2

    def body(s, carry):
        y = yf_ref[s] + pltpu.roll(ym_ref[Tb - 1 - s], half, 1)
        y = y.reshape(2, RWKV_HEAD, LANES)
        mu = jnp.mean(y, axis=1, keepdims=True)
        d = y - mu
        var = jnp.mean(d * d, axis=1, keepdims=True)
        yn = (d * lax.rsqrt(var + RWKV_GN_EPS)).reshape(LANES, LANES) * lng_ref[...] + lnb_ref[...]
        tok = yn.T
        tok_ref[0, pl.ds(s, half, stride=Tb), :] = tok[:half]
        tok_ref[1, pl.ds(Tb - 1 - s, half, stride=Tb), :] = tok[half:]
        return carry

    lax.fori_loop(0, Tb, body, 0, unroll=2)
    nb = of_ref.shape[1]
    for n in range(half):
        p, b = n // nb, n % nb
        rows = slice(n * Tb, (n + 1) * Tb)
        of_ref[p, b] = (tok_ref[0, rows, :] + bonf_ref[p, b]) * gf_ref[p, b]
        om_ref[p, b] = (tok_ref[1, rows, :] + bonm_ref[p, b]) * gm_ref[p, b]


def _rwkv_post(yT, bonus, g, ln_gT, ln_bT, *, Tb):
    P, B, S, _ = bonus.shape
    nC = S // Tb
    nH = nC // 2
    fwd = pl.BlockSpec((P, B, Tb, LANES), lambda c: (0, 0, c, 0))
    mir = pl.BlockSpec((P, B, Tb, LANES), lambda c: (0, 0, nC - 1 - c, 0))
    tab = pl.BlockSpec((LANES, LANES), lambda c: (0, 0))
    out = jax.ShapeDtypeStruct((P, B, S // 2, LANES), F32)
    return pl.pallas_call(
        functools.partial(_rwkv_post_kernel, Tb=Tb),
        grid=(nH,),
        in_specs=[pl.BlockSpec((Tb, LANES, LANES), lambda c: (c, 0, 0)),
                  pl.BlockSpec((Tb, LANES, LANES), lambda c: (nC - 1 - c, 0, 0)),
                  fwd, mir, fwd, mir, tab, tab],
        out_specs=[fwd, pl.BlockSpec((P, B, Tb, LANES), lambda c: (0, 0, nH - 1 - c, 0))],
        out_shape=[out, out],
        scratch_shapes=[pltpu.VMEM((2, P * B * Tb, LANES), F32)],
        compiler_params=_params(("arbitrary",)),
        name="rwkv_groupnorm",
    )(yT, yT, bonus, bonus, g, g, ln_gT, ln_bT)


def _pair_table(p, batch):
    pairs = p.shape[0] // LANES
    t = jnp.repeat(p.reshape(pairs, LANES).T, batch, axis=1)
    return jnp.tile(t, (1, LANES // (pairs * batch)))


NA_GROUP = 4


def _na_kernel(q_ref, k_ref, v_ref, bias_ref, o_ref, *, rows, win):
    GW = GRID_W
    lane = lax.broadcasted_iota(jnp.int32, (2 * GW, LANES), 1)
    head = lax.broadcasted_iota(jnp.int32, (2 * GW, LANES), 0) // GW
    own = (lane // NA_HEAD) == head
    qc = lax.broadcasted_iota(jnp.int32, (2 * GW, win * GW), 0) % GW
    kc = lax.broadcasted_iota(jnp.int32, (2 * GW, win * GW), 1) % GW
    cs = jnp.clip(qc - NA_WIN_C // 2, 0, GW - NA_WIN_C)
    ok = (kc >= cs) & (kc < cs + NA_WIN_C)
    low = lax.broadcasted_iota(jnp.int32, (GW, LANES), 1) < NA_HEAD
    scale = NA_HEAD ** -0.5

    def body(rg, carry):
        scores, vbs, qss = [], [], []
        for i in range(NA_GROUP):
            r = rg * NA_GROUP + i
            r0 = jnp.clip(r - win // 2, 0, rows - win)
            qs = pl.multiple_of(r * GW, GW)
            ks = pl.multiple_of(r0 * GW, GW)
            q = q_ref[0, pl.ds(qs, GW), :] * scale
            qq = jnp.where(own, jnp.concatenate([q, q], axis=0), 0.0).astype(BF16)
            kb = k_ref[0, pl.ds(ks, win * GW), :]
            s = lax.dot_general(qq, kb, (((1,), (1,)), ((), ())), preferred_element_type=F32)
            scores.append(jnp.where(ok, s + bias_ref[0, r - r0], -1e30))
            vbs.append(v_ref[0, pl.ds(ks, win * GW), :])
            qss.append(qs)
        probs, sums = [], []
        for s in scores:
            p = jnp.exp(s - jnp.max(s, axis=-1, keepdims=True))
            sums.append(jnp.sum(p, axis=-1, keepdims=True))
            probs.append(p.astype(BF16))
        for p, l, vb, qs in zip(probs, sums, vbs, qss):
            o = jnp.dot(p, vb, preferred_element_type=F32) / l
            o_ref[0, pl.ds(qs, GW), :] = jnp.where(low, o[:GW], o[GW:]).astype(o_ref.dtype)
        return carry

    lax.fori_loop(0, rows // NA_GROUP, body, 0)


def _na_bias(rpb, win):
    col = jnp.arange(GRID_W)
    dx = jnp.clip(col[None, :] - col[:, None], 1 - NA_WIN_C, NA_WIN_C - 1) + NA_WIN_C - 1
    d = jnp.arange(win)
    p = jnp.arange(win)
    dy = p[None, :] - d[:, None] + NA_WIN_R_MAX - 1
    t = rpb[:, dy][:, :, :, dx]
    H = rpb.shape[0]
    t = t.reshape(H // 2, 2, win, win, GRID_W, GRID_W).transpose(0, 2, 1, 4, 3, 5)
    return t.reshape(H // 2, win, 2 * GRID_W, win * GRID_W).astype(F32)


def _neighbourhood_attention(p, rpb):
    B, S, D3 = p.shape
    D = D3 // 3
    pairs = D // LANES
    rows = S // GRID_W
    win = min(NA_WIN_R_MAX, rows)
    bias = _na_bias(rpb, win)
    return pl.pallas_call(
        functools.partial(_na_kernel, rows=rows, win=win),
        grid=(B, pairs),
        in_specs=[pl.BlockSpec((1, S, LANES), lambda b, h: (b, 0, h)),
                  pl.BlockSpec((1, S, LANES), lambda b, h: (b, 0, pairs + h)),
                  pl.BlockSpec((1, S, LANES), lambda b, h: (b, 0, 2 * pairs + h)),
                  pl.BlockSpec((1, win, 2 * GRID_W, win * GRID_W), lambda b, h: (h, 0, 0, 0))],
        out_specs=pl.BlockSpec((1, S, LANES), lambda b, h: (b, 0, h)),
        out_shape=jax.ShapeDtypeStruct((B, S, D), BF16),
        compiler_params=_params(("arbitrary", "arbitrary")),
        name="neighbourhood_attention",
    )(p, p, p, bias)


def _merge_kernel(x_ref, zc_ref, yr_lo_ref, yr_hi_ref, yn_ref, gate_ref, wc_ref, bc_ref, wr_ref, wn_ref, wo_ref, o_ref,
                  *, seq):
    tm, D = x_ref.shape
    yc = jnp.dot(zc_ref[...].astype(BF16), wc_ref[...], preferred_element_type=F32) + bc_ref[...]
    second_half = (pl.program_id(0) * tm) % seq >= seq // 2
    tiles = lambda ref: jnp.concatenate([ref[q] for q in range(ref.shape[0])], axis=-1)
    yr_in = jnp.where(second_half, tiles(yr_hi_ref), tiles(yr_lo_ref))
    yr = jnp.dot(yr_in.astype(BF16), wr_ref[...], preferred_element_type=F32)
    yn = jnp.dot(yn_ref[...].astype(BF16), wn_ref[...], preferred_element_type=F32)
    merged = gate_ref[:, :D] * yc + gate_ref[:, D:2 * D] * yr + gate_ref[:, 2 * D:] * yn
    o_ref[...] = x_ref[...] + jnp.dot(merged.astype(BF16), wo_ref[...], preferred_element_type=F32)


def _merge(x2, zc, yr_lo, yr_hi, yn, gate, wc, bc, wr, wn, wo, *, tm, seq):
    M, D = x2.shape
    hs = seq // 2
    row = pl.BlockSpec((tm, D), lambda i: (i, 0))
    wsp = pl.BlockSpec((D, D), lambda i: (0, 0))
    half_rows = pl.BlockSpec((D // LANES, tm, LANES), lambda i: (0, (((i * tm) // seq) * hs + (i * tm) % hs) // tm, 0))
    return pl.pallas_call(
        functools.partial(_merge_kernel, seq=seq),
        grid=(M // tm,),
        in_specs=[row, row, half_rows, half_rows, row,
                  pl.BlockSpec((tm, 3 * D), lambda i: (i, 0)),
                  wsp, pl.BlockSpec((1, D), lambda i: (0, 0)), wsp, wsp, wsp],
        out_specs=row,
        out_shape=jax.ShapeDtypeStruct((M, D), F32),
        compiler_params=_params(("arbitrary",)),
        name="gated_merge",
    )(x2, zc, yr_lo, yr_hi, yn, gate, wc, bc, wr, wn, wo)


def _xattn_kernel(x_ref, g_ref, wq_ref, kv_ref, wo_ref, o_ref):
    x = x_ref[0]
    D = x.shape[-1]
    dh = D // XA_HEADS
    h = _rms(x, g_ref[...]).astype(BF16)
    q = jnp.dot(h, wq_ref[...], preferred_element_type=F32)
    outs = []
    for hd in range(XA_HEADS):
        qh = q[:, hd * dh:(hd + 1) * dh].astype(BF16)
        km = kv_ref[0, :, hd * dh:(hd + 1) * dh]
        vm = kv_ref[0, :, D + hd * dh:D + (hd + 1) * dh]
        s = lax.dot_general(qh, km, (((1,), (1,)), ((), ())), preferred_element_type=F32) * dh ** -0.5
        m = jnp.max(s, axis=-1, keepdims=True)
        p = jnp.exp(s - m)
        l = jnp.sum(p, axis=-1, keepdims=True)
        outs.append(jnp.dot(p.astype(BF16), vm, preferred_element_type=F32) / l)
    o = jnp.concatenate(outs, axis=-1).astype(BF16)
    o_ref[0] = x + jnp.dot(o, wo_ref[...], preferred_element_type=F32)


def _cross_attention(x, g, wq, kv, wo, *, tm):
    B, S, D = x.shape
    Mm = kv.shape[1]
    row = pl.BlockSpec((1, tm, D), lambda b, i: (b, i, 0))
    wsp = pl.BlockSpec((D, D), lambda b, i: (0, 0))
    return pl.pallas_call(
        _xattn_kernel,
        grid=(B, S // tm),
        in_specs=[row, pl.BlockSpec((1, D), lambda b, i: (0, 0)), wsp,
                  pl.BlockSpec((1, Mm, 2 * D), lambda b, i: (b, 0, 0)), wsp],
        out_specs=row,
        out_shape=jax.ShapeDtypeStruct((B, S, D), F32),
        compiler_params=_params(("arbitrary", "arbitrary")),
        name="memory_cross_attention",
    )(x, g, wq, kv, wo)


def _mlp_kernel(x_ref, g_ref, w1_ref, w2_ref, gf_ref, o_ref, xn_ref, acc_ref, *, final_norm):
    f = pl.program_id(1)

    @pl.when(f == 0)
    def _():
        xn_ref[...] = _rms(x_ref[...], g_ref[...]).astype(BF16)
        acc_ref[...] = jnp.zeros(acc_ref.shape, F32)

    h = jnp.maximum(jnp.dot(xn_ref[...], w1_ref[...], preferred_element_type=F32), 0.0)
    acc_ref[...] += jnp.dot((h * h).astype(BF16), w2_ref[...], preferred_element_type=F32)

    @pl.when(f == pl.num_programs(1) - 1)
    def _():
        y = x_ref[...] + acc_ref[...]
        if final_norm:
            y = _rms(y, gf_ref[...])
        o_ref[...] = y


def _mlp(x2, g, w1, w2, gf, *, tm, tf, final_norm):
    M, D = x2.shape
    F = w1.shape[1]
    row = pl.BlockSpec((tm, D), lambda i, f: (i, 0))
    vec = pl.BlockSpec((1, D), lambda i, f: (0, 0))
    return pl.pallas_call(
        functools.partial(_mlp_kernel, final_norm=final_norm),
        grid=(M // tm, F // tf),
        in_specs=[row, vec, pl.BlockSpec((D, tf), lambda i, f: (0, f)),
                  pl.BlockSpec((tf, D), lambda i, f: (f, 0)), vec],
        out_specs=row,
        out_shape=jax.ShapeDtypeStruct((M, D), F32),
        scratch_shapes=[pltpu.VMEM((tm, D), BF16), pltpu.VMEM((tm, D), F32)],
        compiler_params=_params(("arbitrary", "arbitrary")),
        name="relu2_mlp",
    )(x2, g, w1, w2, gf)


def _tiles(S):
    return dict(
        proj_tm=1024,
        side_tm=2048,
        side_tn=1024,
        conv_ts=min(512, S),
        nat_tb=min(128, S),
        post_tb=16,
        scan_tc=16,
        merge_tm=256,
        xa_tm=min(512, S),
        mlp_tm=512,
        mlp_tf=1024,
    )


def _pick_tn(n, cap=1280):
    best = LANES
    for t in range(LANES, cap + 1, LANES):
        if n % t == 0:
            best = t
    return best


def kernel(x, mem, norm_mix_g, w_in, gate_b, conv_b_glu, conv_dw_w, conv_dw_b, conv_ln_g, conv_ln_b,
           conv_proj_w, conv_proj_b, rwkv_mu_prev, rwkv_mu_next, rwkv_w0, rwkv_w2, rwkv_a0, rwkv_a2,
           rwkv_g2, rwkv_k_k, rwkv_k_a, rwkv_r_k, rwkv_ln_g, rwkv_ln_b, rwkv_proj_w, na_rpb, na_proj_w,
           w_out, norm_xa_g, norm_mem_g, xa_wq, xa_wkv, xa_wo, norm_mlp_g, mlp_w1, mlp_w2, norm_f_g):
    B, S, D = x.shape
    depth = w_in.shape[0]
    Mm = mem.shape[1]
    T = _tiles(S)
    n_conv = 2 * D
    n_rwkv = 3 * D + 2 * R_DECAY + 2 * R_ICL + R_GATE
    n_na = 3 * D
    off_rwkv = n_conv
    off_na = off_rwkv + n_rwkv
    off_gate = off_na + n_na

    x2 = x.reshape(B * S, D)
    mem2 = mem.reshape(B * Mm, D)
    for l in range(depth):
        w_in_bf = w_in[l].astype(BF16)
        g_mix = norm_mix_g[l][None]

        p_rwkv, xn = _norm_matmul(x2, g_mix, w_in_bf[:, off_rwkv:off_na], jnp.zeros((1, n_rwkv), F32),
                                  tm=T["proj_tm"], tn=_pick_tn(n_rwkv), emit_xn=True, name="proj_rwkv")
        p_rwkv = p_rwkv.reshape(B, S, n_rwkv)

        def proj(c0, c1, name, bias=None, act=None, out_dtype=F32):
            n = c1 - c0
            b = jnp.zeros((1, n), F32) if bias is None else bias[None]
            return _matmul(xn, w_in_bf[:, c0:c1], b, tm=T["side_tm"], tn=T["side_tn"], act=act,
                           out_dtype=out_dtype, name=name)

        p_conv = proj(0, off_rwkv, "proj_conv").reshape(B, S, n_conv)
        p_na = proj(off_na, off_gate, "proj_na", out_dtype=BF16).reshape(B, S, n_na)
        gates = proj(off_gate, off_gate + 3 * D, "proj_gate", bias=gate_b[l], act="sigmoid")

        zc = _conformer_conv(p_conv, conv_b_glu[l], conv_dw_w[l], conv_dw_b[l], conv_ln_g[l], conv_ln_b[l],
                             ts=T["conv_ts"])

        r, k, v, d0, d1, s0, s1, g, bonus = _rwkv_nat(
            p_rwkv, rwkv_mu_prev[l], rwkv_mu_next[l], rwkv_w0[l], rwkv_w2[l], rwkv_a0[l], rwkv_a2[l],
            rwkv_g2[l], rwkv_k_a[l], rwkv_r_k[l], Tb=T["nat_tb"])
        yT = _rwkv_scan(r, k, v, d0, d1, s0, s1, _pair_table(rwkv_k_k[l], B), _pair_table(rwkv_k_a[l], B),
                        Tc=T["scan_tc"])
        yr_lo, yr_hi = _rwkv_post(yT, bonus, g, _pair_table(rwkv_ln_g[l], B), _pair_table(rwkv_ln_b[l], B),
                                  Tb=T["post_tb"])

        y_na = _neighbourhood_attention(p_na, na_rpb[l])

        half_tiles = lambda a: a.reshape(D // LANES, B * S // 2, LANES)
        x2 = _merge(x2, zc.reshape(B * S, D), half_tiles(yr_lo), half_tiles(yr_hi), y_na.reshape(B * S, D),
                    gates, conv_proj_w[l].astype(BF16), conv_proj_b[l][None], rwkv_proj_w[l].astype(BF16),
                    na_proj_w[l].astype(BF16), w_out[l].astype(BF16), tm=T["merge_tm"], seq=S)

        kv = _norm_matmul(mem2, norm_mem_g[l][None], xa_wkv[l].astype(BF16), jnp.zeros((1, 2 * D), F32),
                          tm=min(512, B * Mm), tn=1024, out_dtype=BF16, name="proj_mem_kv")
        xa = _cross_attention(x2.reshape(B, S, D), norm_xa_g[l][None], xa_wq[l].astype(BF16),
                              kv.reshape(B, Mm, 2 * D), xa_wo[l].astype(BF16), tm=T["xa_tm"])

        x2 = _mlp(xa.reshape(B * S, D), norm_mlp_g[l][None], mlp_w1[l].astype(BF16), mlp_w2[l].astype(BF16),
                  norm_f_g[None], tm=T["mlp_tm"], tf=T["mlp_tf"], final_norm=(l == depth - 1))
    return x2.reshape(B, S, D)
```

```python
import functools
import math

import jax
import jax.numpy as jnp
from jax import lax
from jax.experimental import pallas as pl
from jax.experimental.pallas import tpu as pltpu

F32 = jnp.float32
BF16 = jnp.bfloat16

NORM_EPS = 1e-6
LN_EPS = 1e-5
GRID_W = 64
CONV_W = 31
RWKV_HEAD = 64
R_DECAY = 64
R_ICL = 64
R_GATE = 128
RWKV_GN_EPS = 1e-5 * RWKV_HEAD
NA_HEAD = 64
NA_WIN_R_MAX = 8
NA_WIN_C = 16
XA_HEADS = 4
LANES = 128
VMEM_LIMIT = 48 * 1024 * 1024
EXP_NEG_HALF = math.exp(-0.5)


def _sig(x):
    return 1.0 / (1.0 + jnp.exp(-x))


def _rms(x, g):
    ms = jnp.mean(x * x, axis=-1, keepdims=True)
    return x * lax.rsqrt(ms + NORM_EPS) * g


def _params(sem):
    return pltpu.CompilerParams(dimension_semantics=sem, vmem_limit_bytes=VMEM_LIMIT)


def _norm_matmul_kernel(x_ref, g_ref, w_ref, b_ref, o_ref, xn_ref):
    @pl.when(pl.program_id(1) == 0)
    def _():
        xn_ref[...] = _rms(x_ref[...], g_ref[...]).astype(BF16)

    y = jnp.dot(xn_ref[...], w_ref[...], preferred_element_type=F32) + b_ref[...]
    o_ref[...] = y.astype(o_ref.dtype)


def _norm_matmul(x2, g, w_bf, bias, *, tm, tn, out_dtype=F32, emit_xn=False, name):
    M, K = x2.shape
    N = w_bf.shape[1]
    out_specs = pl.BlockSpec((tm, tn), lambda i, j: (i, j))
    out_shape = jax.ShapeDtypeStruct((M, N), out_dtype)
    scratch = [pltpu.VMEM((tm, K), BF16)]
    if emit_xn:
        out_specs = [out_specs, pl.BlockSpec((tm, K), lambda i, j: (i, 0))]
        out_shape = [out_shape, jax.ShapeDtypeStruct((M, K), BF16)]
        scratch = []
    return pl.pallas_call(
        _norm_matmul_kernel,
        grid=(M // tm, N // tn),
        in_specs=[pl.BlockSpec((tm, K), lambda i, j: (i, 0)),
                  pl.BlockSpec((1, K), lambda i, j: (0, 0)),
                  pl.BlockSpec((K, tn), lambda i, j: (0, j)),
                  pl.BlockSpec((1, tn), lambda i, j: (0, j))],
        out_specs=out_specs,
        out_shape=out_shape,
        scratch_shapes=scratch,
        compiler_params=_params(("arbitrary", "arbitrary")),
        name=name,
    )(x2, g, w_bf, bias)


def _matmul_kernel(x_ref, w_ref, b_ref, o_ref, *, act):
    y = jnp.dot(x_ref[...], w_ref[...], preferred_element_type=F32) + b_ref[...]
    if act == "sigmoid":
        y = _sig(y)
    o_ref[...] = y.astype(o_ref.dtype)


def _matmul(xn, w_bf, bias, *, tm, tn, act=None, out_dtype=F32, name):
    M, K = xn.shape
    N = w_bf.shape[1]
    return pl.pallas_call(
        functools.partial(_matmul_kernel, act=act),
        grid=(M // tm, N // tn),
        in_specs=[pl.BlockSpec((tm, K), lambda i, j: (i, 0)),
                  pl.BlockSpec((K, tn), lambda i, j: (0, j)),
                  pl.BlockSpec((1, tn), lambda i, j: (0, j))],
        out_specs=pl.BlockSpec((tm, tn), lambda i, j: (i, j)),
        out_shape=jax.ShapeDtypeStruct((M, N), out_dtype),
        compiler_params=_params(("arbitrary", "arbitrary")),
        name=name,
    )(xn, w_bf, bias)


CONV_HALO = 16
CONV_ROWS = 128
CONV_LANES = 128


def _conv_kernel(cur_ref, prev_ref, next_ref, bglu_ref, dww_ref, dwb_ref, lng_ref, lnb_ref, o_ref, zp_ref, cz_ref,
                 *, ts, C):
    i = pl.program_id(1)
    n = pl.num_programs(1)
    H = CONV_HALO
    bg = bglu_ref[...]

    def glu(u):
        return (u[:, :C] + bg[:, :C]) * _sig(u[:, C:] + bg[:, C:])

    zp_ref[H:H + ts, :] = glu(cur_ref[0])
    zp_ref[0:H, :] = jnp.where(i > 0, glu(prev_ref[0]), 0.0)
    zp_ref[H + ts:H + ts + H, :] = jnp.where(i < n - 1, glu(next_ref[0]), 0.0)

    R, LC = CONV_ROWS, CONV_LANES
    first = H - CONV_W // 2

    def conv_body(rc, carry):
        r0 = pl.multiple_of(rc * R, R)
        for lc in range(C // LC):
            ls = slice(lc * LC, (lc + 1) * LC)
            win = zp_ref[pl.ds(r0, R + 2 * H), ls]
            acc = jnp.zeros((R, LC), F32) + dwb_ref[:, ls]
            for rho in range(8):
                part = None
                for k in range(CONV_W):
                    if (first + k) % 8 != rho:
                        continue
                    term = win[first + k:first + k + R, :] * dww_ref[k:k + 1, ls]
                    part = term if part is None else part + term
                if part is not None:
                    acc = acc + part
            cz_ref[pl.ds(r0, R), ls] = acc
        return carry

    lax.fori_loop(0, ts // R, conv_body, 0)

    def ln_body(rc, carry):
        r0 = pl.multiple_of(rc * R, R)
        z = cz_ref[pl.ds(r0, R), :]
        mu = jnp.mean(z, axis=-1, keepdims=True)
        d = z - mu
        var = jnp.mean(d * d, axis=-1, keepdims=True)
        y = d * lax.rsqrt(var + LN_EPS) * lng_ref[...] + lnb_ref[...]
        o_ref[0, pl.ds(r0, R), :] = (y * _sig(y)).astype(o_ref.dtype)
        return carry

    lax.fori_loop(0, ts // R, ln_body, 0, unroll=2)


def _conformer_conv(u, b_glu, dw_w, dw_b, ln_g, ln_b, *, ts):
    B, S, C2 = u.shape
    C = C2 // 2
    H = CONV_HALO
    nh = ts // H
    last = S // H - 1
    dww = jnp.zeros((32, C), F32).at[:CONV_W].set(dw_w)
    return pl.pallas_call(
        functools.partial(_conv_kernel, ts=ts, C=C),
        grid=(B, S // ts),
        in_specs=[pl.BlockSpec((1, ts, C2), lambda b, i: (b, i, 0)),
                  pl.BlockSpec((1, H, C2), lambda b, i: (b, jnp.maximum(i * nh - 1, 0), 0)),
                  pl.BlockSpec((1, H, C2), lambda b, i: (b, jnp.minimum((i + 1) * nh, last), 0)),
                  pl.BlockSpec((1, C2), lambda b, i: (0, 0)),
                  pl.BlockSpec((32, C), lambda b, i: (0, 0)),
                  pl.BlockSpec((1, C), lambda b, i: (0, 0)),
                  pl.BlockSpec((1, C), lambda b, i: (0, 0)),
                  pl.BlockSpec((1, C), lambda b, i: (0, 0))],
        out_specs=pl.BlockSpec((1, ts, C), lambda b, i: (b, i, 0)),
        out_shape=jax.ShapeDtypeStruct((B, S, C), BF16),
        scratch_shapes=[pltpu.VMEM((ts + 2 * H, C), F32), pltpu.VMEM((ts, C), F32)],
        compiler_params=_params(("arbitrary", "arbitrary")),
        name="conformer_conv",
    )(u, u, u, b_glu[None], dww, dw_b[None], ln_g[None], ln_b[None])


def _rwkv_nat_kernel(cur_ref, prev_ref, next_ref, mup_ref, mun_ref, w0_ref, w2_ref, a0_ref, a2_ref, g2_ref,
                     ka_ref, rk_ref, bd_ref,
                     r_o, k_o, v_o, d0_o, d1_o, s0_o, s1_o, g_o, bon_o, *, Tb, D):
    i = pl.program_id(1)
    n = pl.num_programs(1)
    rows = lax.broadcasted_iota(jnp.int32, (Tb, LANES), 0)
    first_row = rows == 0
    last_row = rows == Tb - 1
    has_prev = i > 0
    has_next = i < n - 1

    def shifted(c0, c1):
        outs = []
        for c in range(c0, c1, LANES):
            cs = slice(c, c + LANES)
            x = cur_ref[0, :, cs]
            prow = jnp.where(has_prev, prev_ref[0, 7:8, cs], 0.0)
            nrow = jnp.where(has_next, next_ref[0, 0:1, cs], 0.0)
            xp = jnp.where(first_row, prow, pltpu.roll(x, 1, 0))
            xn = jnp.where(last_row, nrow, pltpu.roll(x, Tb - 1, 0))
            outs.append(x + mup_ref[:, cs] * (xp - x) + mun_ref[:, cs] * (xn - x))
        return outs[0] if len(outs) == 1 else jnp.concatenate(outs, axis=-1)

    r = shifted(0, D)
    k = shifted(D, 2 * D)
    v = shifted(2 * D, 3 * D)
    wd = shifted(3 * D, 3 * D + 2 * R_DECAY)
    ad = shifted(3 * D + 2 * R_DECAY, 3 * D + 2 * R_DECAY + 2 * R_ICL)
    gd = shifted(3 * D + 2 * R_DECAY + 2 * R_ICL, 3 * D + 2 * R_DECAY + 2 * R_ICL + R_GATE)

    zw = w0_ref[...] + jnp.dot(jnp.tanh(wd).astype(BF16), w2_ref[...], preferred_element_type=F32)
    dec = jnp.exp(-EXP_NEG_HALF * _sig(zw))
    asig = _sig(a0_ref[...] + jnp.dot(ad.astype(BF16), a2_ref[...], preferred_element_type=F32))
    g = jnp.dot(_sig(gd).astype(BF16), g2_ref[...], preferred_element_type=F32)

    s0 = asig[:, :D]
    s1 = asig[:, D:]
    e = r * k * (2.0 + (s0 + s1 - 2.0) * ka_ref[...]) * rk_ref[...]
    parts = []
    for c in range(0, D, LANES):
        et = e[:, c:c + LANES]
        hi = et.astype(BF16)
        lo = (et - hi.astype(F32)).astype(BF16)
        parts.append(jnp.dot(hi, bd_ref[...], preferred_element_type=F32)
                     + jnp.dot(lo, bd_ref[...], preferred_element_type=F32))
    rk_b = jnp.concatenate(parts, axis=-1)

    def store_tiles(o_ref, val):
        for q in range(D // LANES):
            o_ref[q, 0] = val[:, q * LANES:(q + 1) * LANES]

    store_tiles(r_o, r)
    store_tiles(k_o, k)
    store_tiles(v_o, v)
    store_tiles(d0_o, dec[:, :D])
    store_tiles(d1_o, dec[:, D:])
    store_tiles(s0_o, s0)
    store_tiles(s1_o, s1)
    store_tiles(g_o, g)
    store_tiles(bon_o, rk_b * v)


def _rwkv_nat(p, mu_prev, mu_next, w0, w2, a0, a2, g2, k_a, r_k, *, Tb):
    B, S, NR = p.shape
    D = (NR - 2 * R_DECAY - 2 * R_ICL - R_GATE) // 3
    nh = Tb // 8
    last = S // 8 - 1
    zero = jnp.zeros((R_DECAY, D), F32)
    w2bd = jnp.concatenate([jnp.concatenate([w2[0], zero], 1), jnp.concatenate([zero, w2[1]], 1)], 0).astype(BF16)
    a2bd = jnp.concatenate([jnp.concatenate([a2[0], zero], 1), jnp.concatenate([zero, a2[1]], 1)], 0).astype(BF16)
    half = jnp.arange(LANES) // RWKV_HEAD
    bd = (half[:, None] == half[None, :]).astype(BF16)
    full = lambda shape: pl.BlockSpec(shape, lambda b, i: (0,) * len(shape))
    out_spec = pl.BlockSpec((D // LANES, 1, Tb, LANES), lambda b, i: (0, b, i, 0))
    out = jax.ShapeDtypeStruct((D // LANES, B, S, LANES), F32)
    return pl.pallas_call(
        functools.partial(_rwkv_nat_kernel, Tb=Tb, D=D),
        grid=(B, S // Tb),
        in_specs=[pl.BlockSpec((1, Tb, NR), lambda b, i: (b, i, 0)),
                  pl.BlockSpec((1, 8, NR), lambda b, i: (b, jnp.maximum(i * nh - 1, 0), 0)),
                  pl.BlockSpec((1, 8, NR), lambda b, i: (b, jnp.minimum((i + 1) * nh, last), 0)),
                  full((1, NR)), full((1, NR)),
                  full((1, 2 * D)), full((2 * R_DECAY, 2 * D)),
                  full((1, 2 * D)), full((2 * R_ICL, 2 * D)),
                  full((R_GATE, D)), full((1, D)), full((1, D)), full((LANES, LANES))],
        out_specs=[out_spec] * 9,
        out_shape=[out] * 9,
        compiler_params=_params(("arbitrary", "arbitrary")),
        name="rwkv_tokens",
    )(p, p, p, mu_prev[None], mu_next[None], w0.reshape(1, 2 * D), w2bd, a0.reshape(1, 2 * D), a2bd,
      g2.astype(BF16), k_a[None], r_k.reshape(1, D), bd)


SCAN_PLANES = 32
SCAN_ACC = 2
SCAN_GROUP = 4
SCAN_ROW_PAD = 8
OP_R, OP_K, OP_V, OP_A, OP_B = range(5)


def _scan_kernel(rf, rm, kf, km, vf, vm, df, dm, sf, sm_, kk_ref, ka_ref, y_ref, P_ref, ops_ref, in_ref, c_ref,
                 *, Tc):
    @pl.when(pl.program_id(0) == 0)
    def _():
        P_ref[...] = jnp.zeros(P_ref.shape, F32)

    c_ref[...] = jnp.ones(c_ref.shape, F32)

    HD = RWKV_HEAD
    half = LANES // 2
    zeros = jnp.zeros((HD, LANES), F32)
    kkT = kk_ref[...]
    kaT = ka_ref[...]
    srcs = (rf, rm, kf, km, vf, vm, df, dm, sf, sm_)
    RF, RM, KF, KM, VF, VM, DF, DM, SF, SM = range(len(srcs))
    RS = Tc + SCAN_ROW_PAD
    for qi, src in enumerate(srcs):
        for n in range(half):
            in_ref[qi, n * RS:n * RS + Tc, :] = src[n // src.shape[1], n % src.shape[1]]

    def stacked(f, m, s):
        top = in_ref[f, pl.ds(s, half, stride=RS), :]
        bot = in_ref[m, pl.ds(Tc - 1 - s, half, stride=RS), :]
        return jnp.concatenate([top, bot], axis=0).T

    def build(s, slot):
        kT = stacked(KF, KM, s)
        sT = stacked(SF, SM, s)
        kk = (kT * kkT).reshape(2, HD, LANES)
        nrm = jnp.sqrt(jnp.sum(kk * kk, axis=1, keepdims=True))
        kk = (kk / jnp.maximum(nrm, 1e-12)).reshape(LANES, LANES)
        c_prev = c_ref[...]
        c = c_prev * stacked(DF, DM, s)
        c_ref[...] = c
        c_inv = 1.0 / c
        ops_ref[slot, OP_R] = stacked(RF, RM, s) * c
        ops_ref[slot, OP_K] = kT * (1.0 + (sT - 1.0) * kaT) * c_inv
        ops_ref[slot, OP_V] = stacked(VF, VM, s)
        ops_ref[slot, OP_A] = -kk * c_prev
        ops_ref[slot, OP_B] = kk * sT * c_inv

    def sweep(s, slot, unscale):
        def op_row(idx, row):
            return ops_ref[slot, idx, pl.ds(row, 1), :]

        for g in range(LANES // HD):
            def sa_body(jb, sas):
                sas = list(sas)
                for jj in range(SCAN_PLANES):
                    row = g * HD + jb * SCAN_PLANES + jj
                    sas[jj % SCAN_ACC] = sas[jj % SCAN_ACC] + P_ref[row] * op_row(OP_A, row)
                return tuple(sas)

            sa = sum(lax.fori_loop(0, HD // SCAN_PLANES, sa_body, (zeros,) * SCAN_ACC))
            v = ops_ref[slot, OP_V, g * HD:(g + 1) * HD, :]

            def upd_body(jb, ys):
                ys = list(ys)
                for jj in range(SCAN_PLANES):
                    row = g * HD + jb * SCAN_PLANES + jj
                    pn = (P_ref[row] + sa * op_row(OP_B, row)) + v * op_row(OP_K, row)
                    ys[jj % SCAN_ACC] = ys[jj % SCAN_ACC] + pn * op_row(OP_R, row)
                    P_ref[row] = pn * c_ref[pl.ds(row, 1), :] if unscale else pn
                return tuple(ys)

            y_ref[s, g * HD:(g + 1) * HD, :] = sum(lax.fori_loop(0, HD // SCAN_PLANES, upd_body, (zeros,) * SCAN_ACC))

    def step_group(sg, last):
        s0 = sg * SCAN_GROUP
        for u in range(SCAN_GROUP):
            build(s0 + u, u)
        for u in range(SCAN_GROUP):
            sweep(s0 + u, u, last and u == SCAN_GROUP - 1)

    groups = Tc // SCAN_GROUP

    def group_body(sg, carry):
        step_group(sg, False)
        return carry

    lax.fori_loop(0, groups - 1, group_body, 0)
    step_group(groups - 1, True)


def _rwkv_scan(r, k, v, d0, d1, s0, s1, k_kT, k_aT, *, Tc):
    P, B, S, _ = r.shape
    nC = S // Tc
    fwd = pl.BlockSpec((P, B, Tc, LANES), lambda c: (0, 0, c, 0))
    mir = pl.BlockSpec((P, B, Tc, LANES), lambda c: (0, 0, nC - 1 - c, 0))
    tab = pl.BlockSpec((LANES, LANES), lambda c: (0, 0))
    return pl.pallas_call(
        functools.partial(_scan_kernel, Tc=Tc),
        grid=(nC,),
        in_specs=[fwd, mir, fwd, mir, fwd, mir, fwd, mir, fwd, mir, tab, tab],
        out_specs=pl.BlockSpec((Tc, LANES, LANES), lambda c: (c, 0, 0)),
        out_shape=jax.ShapeDtypeStruct((S, LANES, LANES), F32),
        scratch_shapes=[pltpu.VMEM((LANES, RWKV_HEAD, LANES), F32), pltpu.VMEM((SCAN_GROUP, 5, LANES, LANES), F32),
                        pltpu.VMEM((10, P * B * (Tc + SCAN_ROW_PAD), LANES), F32), pltpu.VMEM((LANES, LANES), F32)],
        compiler_params=_params(("arbitrary",)),
        name="rwkv_scan",
    )(r, r, k, k, v, v, d0, d1, s0, s1, k_kT, k_aT)


def _rwkv_post_kernel(yf_ref, ym_ref, bonf_ref, bonm_ref, gf_ref, gm_ref, lng_ref, lnb_ref, of_ref, om_ref, tok_ref,
                      *, Tb):
    half = LANES // 2

    def body(s, carry):
        y = yf_ref[s] + pltpu.roll(ym_ref[Tb - 1 - s], half, 1)
        y = y.reshape(2, RWKV_HEAD, LANES)
        mu = jnp.mean(y, axis=1, keepdims=True)
        d = y - mu
        var = jnp.mean(d * d, axis=1, keepdims=True)
        yn = (d * lax.rsqrt(var + RWKV_GN_EPS)).reshape(LANES, LANES) * lng_ref[...] + lnb_ref[...]
        tok = yn.T
        tok_ref[0, pl.ds(s, half, stride=Tb), :] = tok[:half]
        tok_ref[1, pl.ds(Tb - 1 - s, half, stride=Tb), :] = tok[half:]
        return carry

    lax.fori_loop(0, Tb, body, 0, unroll=2)
    nb = of_ref.shape[1]
    for n in range(half):
        p, b = n // nb, n % nb
        rows = slice(n * Tb, (n + 1) * Tb)
        of_ref[p, b] = (tok_ref[0, rows, :] + bonf_ref[p, b]) * gf_ref[p, b]
        om_ref[p, b] = (tok_ref[1, rows, :] + bonm_ref[p, b]) * gm_ref[p, b]


def _rwkv_post(yT, bonus, g, ln_gT, ln_bT, *, Tb):
    P, B, S, _ = bonus.shape
    nC = S // Tb
    nH = nC // 2
    fwd = pl.BlockSpec((P, B, Tb, LANES), lambda c: (0, 0, c, 0))
    mir = pl.BlockSpec((P, B, Tb, LANES), lambda c: (0, 0, nC - 1 - c, 0))
    tab = pl.BlockSpec((LANES, LANES), lambda c: (0, 0))
    out = jax.ShapeDtypeStruct((P, B, S // 2, LANES), F32)
    return pl.pallas_call(
        functools.partial(_rwkv_post_kernel, Tb=Tb),
        grid=(nH,),
        in_specs=[pl.BlockSpec((Tb, LANES, LANES), lambda c: (c, 0, 0)),
                  pl.BlockSpec((Tb, LANES, LANES), lambda c: (nC - 1 - c, 0, 0)),
                  fwd, mir, fwd, mir, tab, tab],
        out_specs=[fwd, pl.BlockSpec((P, B, Tb, LANES), lambda c: (0, 0, nH - 1 - c, 0))],
        out_shape=[out, out],
        scratch_shapes=[pltpu.VMEM((2, P * B * Tb, LANES), F32)],
        compiler_params=_params(("arbitrary",)),
        name="rwkv_groupnorm",
    )(yT, yT, bonus, bonus, g, g, ln_gT, ln_bT)


def _pair_table(p, batch):
    pairs = p.shape[0] // LANES
    t = jnp.repeat(p.reshape(pairs, LANES).T, batch, axis=1)
    return jnp.tile(t, (1, LANES // (pairs * batch)))


NA_GROUP = 4


def _na_kernel(q_ref, k_ref, v_ref, bias_ref, o_ref, *, rows, win):
    GW = GRID_W
    lane = lax.broadcasted_iota(jnp.int32, (2 * GW, LANES), 1)
    head = lax.broadcasted_iota(jnp.int32, (2 * GW, LANES), 0) // GW
    own = (lane // NA_HEAD) == head
    qc = lax.broadcasted_iota(jnp.int32, (2 * GW, win * GW), 0) % GW
    kc = lax.broadcasted_iota(jnp.int32, (2 * GW, win * GW), 1) % GW
    cs = jnp.clip(qc - NA_WIN_C // 2, 0, GW - NA_WIN_C)
    ok = (kc >= cs) & (kc < cs + NA_WIN_C)
    low = lax.broadcasted_iota(jnp.int32, (GW, LANES), 1) < NA_HEAD
    scale = NA_HEAD ** -0.5

    def body(rg, carry):
        scores, vbs, qss = [], [], []
        for i in range(NA_GROUP):
            r = rg * NA_GROUP + i
            r0 = jnp.clip(r - win // 2, 0, rows - win)
            qs = pl.multiple_of(r * GW, GW)
            ks = pl.multiple_of(r0 * GW, GW)
            q = q_ref[0, pl.ds(qs, GW), :] * scale
            qq = jnp.where(own, jnp.concatenate([q, q], axis=0), 0.0).astype(BF16)
            kb = k_ref[0, pl.ds(ks, win * GW), :]
            s = lax.dot_general(qq, kb, (((1,), (1,)), ((), ())), preferred_element_type=F32)
            scores.append(jnp.where(ok, s + bias_ref[0, r - r0], -1e30))
            vbs.append(v_ref[0, pl.ds(ks, win * GW), :])
            qss.append(qs)
        probs, sums = [], []
        for s in scores:
            p = jnp.exp(s - jnp.max(s, axis=-1, keepdims=True))
            sums.append(jnp.sum(p, axis=-1, keepdims=True))
            probs.append(p.astype(BF16))
        for p, l, vb, qs in zip(probs, sums, vbs, qss):
            o = jnp.dot(p, vb, preferred_element_type=F32) / l
            o_ref[0, pl.ds(qs, GW), :] = jnp.where(low, o[:GW], o[GW:]).astype(o_ref.dtype)
        return carry

    lax.fori_loop(0, rows // NA_GROUP, body, 0)


def _na_bias(rpb, win):
    col = jnp.arange(GRID_W)
    dx = jnp.clip(col[None, :] - col[:, None], 1 - NA_WIN_C, NA_WIN_C - 1) + NA_WIN_C - 1
    d = jnp.arange(win)
    p = jnp.arange(win)
    dy = p[None, :] - d[:, None] + NA_WIN_R_MAX - 1
    t = rpb[:, dy][:, :, :, dx]
    H = rpb.shape[0]
    t = t.reshape(H // 2, 2, win, win, GRID_W, GRID_W).transpose(0, 2, 1, 4, 3, 5)
    return t.reshape(H // 2, win, 2 * GRID_W, win * GRID_W).astype(F32)


def _neighbourhood_attention(p, rpb):
    B, S, D3 = p.shape
    D = D3 // 3
    pairs = D // LANES
    rows = S // GRID_W
    win = min(NA_WIN_R_MAX, rows)
    bias = _na_bias(rpb, win)
    return pl.pallas_call(
        functools.partial(_na_kernel, rows=rows, win=win),
        grid=(B, pairs),
        in_specs=[pl.BlockSpec((1, S, LANES), lambda b, h: (b, 0, h)),
                  pl.BlockSpec((1, S, LANES), lambda b, h: (b, 0, pairs + h)),
                  pl.BlockSpec((1, S, LANES), lambda b, h: (b, 0, 2 * pairs + h)),
                  pl.BlockSpec((1, win, 2 * GRID_W, win * GRID_W), lambda b, h: (h, 0, 0, 0))],
        out_specs=pl.BlockSpec((1, S, LANES), lambda b, h: (b, 0, h)),
        out_shape=jax.ShapeDtypeStruct((B, S, D), BF16),
        compiler_params=_params(("arbitrary", "arbitrary")),
        name="neighbourhood_attention",
    )(p, p, p, bias)


def _merge_kernel(x_ref, xn_ref, zc_ref, yr_lo_ref, yr_hi_ref, yn_ref, wg_ref, bg_ref, wc_ref, bc_ref, wr_ref, wn_ref,
                  wo_ref, o_ref, *, seq):
    tm, D = x_ref.shape
    gate = lambda k: _sig(jnp.dot(xn_ref[...], wg_ref[:, k * D:(k + 1) * D], preferred_element_type=F32)
                          + bg_ref[:, k * D:(k + 1) * D])
    yc = jnp.dot(zc_ref[...].astype(BF16), wc_ref[...], preferred_element_type=F32) + bc_ref[...]
    second_half = (pl.program_id(0) * tm) % seq >= seq // 2
    tiles = lambda ref: jnp.concatenate([ref[q] for q in range(ref.shape[0])], axis=-1)
    yr_in = jnp.where(second_half, tiles(yr_hi_ref), tiles(yr_lo_ref))
    yr = jnp.dot(yr_in.astype(BF16), wr_ref[...], preferred_element_type=F32)
    yn = jnp.dot(yn_ref[...].astype(BF16), wn_ref[...], preferred_element_type=F32)
    merged = gate(0) * yc + gate(1) * yr + gate(2) * yn
    o_ref[...] = x_ref[...] + jnp.dot(merged.astype(BF16), wo_ref[...], preferred_element_type=F32)


def _merge(x2, xn, zc, yr_lo, yr_hi, yn, wg, bg, wc, bc, wr, wn, wo, *, tm, seq):
    M, D = x2.shape
    hs = seq // 2
    row = pl.BlockSpec((tm, D), lambda i: (i, 0))
    wsp = pl.BlockSpec((D, D), lambda i: (0, 0), pipeline_mode=pl.Buffered(1))
    half_rows = pl.BlockSpec((D // LANES, tm, LANES), lambda i: (0, (((i * tm) // seq) * hs + (i * tm) % hs) // tm, 0))
    return pl.pallas_call(
        functools.partial(_merge_kernel, seq=seq),
        grid=(M // tm,),
        in_specs=[row, row, row, half_rows, half_rows, row,
                  pl.BlockSpec((D, 3 * D), lambda i: (0, 0), pipeline_mode=pl.Buffered(1)),
                  pl.BlockSpec((1, 3 * D), lambda i: (0, 0)),
                  wsp, pl.BlockSpec((1, D), lambda i: (0, 0)), wsp, wsp, wsp],
        out_specs=row,
        out_shape=jax.ShapeDtypeStruct((M, D), F32),
        compiler_params=_params(("arbitrary",)),
        name="gated_merge",
    )(x2, xn, zc, yr_lo, yr_hi, yn, wg, bg, wc, bc, wr, wn, wo)


def _xattn_kernel(x_ref, g_ref, wq_ref, kv_ref, wo_ref, o_ref):
    x = x_ref[0]
    D = x.shape[-1]
    dh = D // XA_HEADS
    h = _rms(x, g_ref[...]).astype(BF16)
    q = jnp.dot(h, wq_ref[...], preferred_element_type=F32)
    outs = []
    for hd in range(XA_HEADS):
        qh = q[:, hd * dh:(hd + 1) * dh].astype(BF16)
        km = kv_ref[0, :, hd * dh:(hd + 1) * dh]
        vm = kv_ref[0, :, D + hd * dh:D + (hd + 1) * dh]
        s = lax.dot_general(qh, km, (((1,), (1,)), ((), ())), preferred_element_type=F32) * dh ** -0.5
        m = jnp.max(s, axis=-1, keepdims=True)
        p = jnp.exp(s - m)
        l = jnp.sum(p, axis=-1, keepdims=True)
        outs.append(jnp.dot(p.astype(BF16), vm, preferred_element_type=F32) / l)
    o = jnp.concatenate(outs, axis=-1).astype(BF16)
    o_ref[0] = x + jnp.dot(o, wo_ref[...], preferred_element_type=F32)


def _cross_attention(x, g, wq, kv, wo, *, tm):
    B, S, D = x.shape
    Mm = kv.shape[1]
    row = pl.BlockSpec((1, tm, D), lambda b, i: (b, i, 0))
    wsp = pl.BlockSpec((D, D), lambda b, i: (0, 0))
    return pl.pallas_call(
        _xattn_kernel,
        grid=(B, S // tm),
        in_specs=[row, pl.BlockSpec((1, D), lambda b, i: (0, 0)), wsp,
                  pl.BlockSpec((1, Mm, 2 * D), lambda b, i: (b, 0, 0)), wsp],
        out_specs=row,
        out_shape=jax.ShapeDtypeStruct((B, S, D), F32),
        compiler_params=_params(("arbitrary", "arbitrary")),
        name="memory_cross_attention",
    )(x, g, wq, kv, wo)


def _mlp_kernel(x_ref, g_ref, w1_ref, w2_ref, gf_ref, o_ref, xn_ref, acc_ref, *, final_norm):
    f = pl.program_id(1)

    @pl.when(f == 0)
    def _():
        xn_ref[...] = _rms(x_ref[...], g_ref[...]).astype(BF16)
        acc_ref[...] = jnp.zeros(acc_ref.shape, F32)

    h = jnp.maximum(jnp.dot(xn_ref[...], w1_ref[...], preferred_element_type=F32), 0.0)
    acc_ref[...] += jnp.dot((h * h).astype(BF16), w2_ref[...], preferred_element_type=F32)

    @pl.when(f == pl.num_programs(1) - 1)
    def _():
        y = x_ref[...] + acc_ref[...]
        if final_norm:
            y = _rms(y, gf_ref[...])
        o_ref[...] = y


def _mlp(x2, g, w1, w2, gf, *, tm, tf, final_norm):
    M, D = x2.shape
    F = w1.shape[1]
    row = pl.BlockSpec((tm, D), lambda i, f: (i, 0))
    vec = pl.BlockSpec((1, D), lambda i, f: (0, 0))
    return pl.pallas_call(
        functools.partial(_mlp_kernel, final_norm=final_norm),
        grid=(M // tm, F // tf),
        in_specs=[row, vec, pl.BlockSpec((D, tf), lambda i, f: (0, f)),
                  pl.BlockSpec((tf, D), lambda i, f: (f, 0)), vec],
        out_specs=row,
        out_shape=jax.ShapeDtypeStruct((M, D), F32),
        scratch_shapes=[pltpu.VMEM((tm, D), BF16), pltpu.VMEM((tm, D), F32)],
        compiler_params=_params(("arbitrary", "arbitrary")),
        name="relu2_mlp",
    )(x2, g, w1, w2, gf)


def _tiles(S):
    return dict(
        proj_tm=1024,
        side_tm=2048,
        side_tn=1024,
        conv_ts=min(512, S),
        nat_tb=min(128, S),
        post_tb=16,
        scan_tc=16,
        merge_tm=min(512, S // 2),
        xa_tm=min(512, S),
        mlp_tm=512,
        mlp_tf=1024,
    )


def _pick_tn(n, cap=1280):
    best = LANES
    for t in range(LANES, cap + 1, LANES):
        if n % t == 0:
            best = t
    return best


def kernel(x, mem, norm_mix_g, w_in, gate_b, conv_b_glu, conv_dw_w, conv_dw_b, conv_ln_g, conv_ln_b,
           conv_proj_w, conv_proj_b, rwkv_mu_prev, rwkv_mu_next, rwkv_w0, rwkv_w2, rwkv_a0, rwkv_a2,
           rwkv_g2, rwkv_k_k, rwkv_k_a, rwkv_r_k, rwkv_ln_g, rwkv_ln_b, rwkv_proj_w, na_rpb, na_proj_w,
           w_out, norm_xa_g, norm_mem_g, xa_wq, xa_wkv, xa_wo, norm_mlp_g, mlp_w1, mlp_w2, norm_f_g):
    B, S, D = x.shape
    depth = w_in.shape[0]
    Mm = mem.shape[1]
    T = _tiles(S)
    n_conv = 2 * D
    n_rwkv = 3 * D + 2 * R_DECAY + 2 * R_ICL + R_GATE
    n_na = 3 * D
    off_rwkv = n_conv
    off_na = off_rwkv + n_rwkv
    off_gate = off_na + n_na

    x2 = x.reshape(B * S, D)
    mem2 = mem.reshape(B * Mm, D)
    for l in range(depth):
        w_in_bf = w_in[l].astype(BF16)
        g_mix = norm_mix_g[l][None]

        p_rwkv, xn = _norm_matmul(x2, g_mix, w_in_bf[:, off_rwkv:off_na], jnp.zeros((1, n_rwkv), F32),
                                  tm=T["proj_tm"], tn=_pick_tn(n_rwkv), emit_xn=True, name="proj_rwkv")
        p_rwkv = p_rwkv.reshape(B, S, n_rwkv)

        def proj(c0, c1, name, bias=None, act=None, out_dtype=F32):
            n = c1 - c0
            b = jnp.zeros((1, n), F32) if bias is None else bias[None]
            return _matmul(xn, w_in_bf[:, c0:c1], b, tm=T["side_tm"], tn=T["side_tn"], act=act,
                           out_dtype=out_dtype, name=name)

        p_conv = proj(0, off_rwkv, "proj_conv").reshape(B, S, n_conv)
        p_na = proj(off_na, off_gate, "proj_na", out_dtype=BF16).reshape(B, S, n_na)

        zc = _conformer_conv(p_conv, conv_b_glu[l], conv_dw_w[l], conv_dw_b[l], conv_ln_g[l], conv_ln_b[l],
                             ts=T["conv_ts"])

        r, k, v, d0, d1, s0, s1, g, bonus = _rwkv_nat(
            p_rwkv, rwkv_mu_prev[l], rwkv_mu_next[l], rwkv_w0[l], rwkv_w2[l], rwkv_a0[l], rwkv_a2[l],
            rwkv_g2[l], rwkv_k_a[l], rwkv_r_k[l], Tb=T["nat_tb"])
        yT = _rwkv_scan(r, k, v, d0, d1, s0, s1, _pair_table(rwkv_k_k[l], B), _pair_table(rwkv_k_a[l], B),
                        Tc=T["scan_tc"])
        yr_lo, yr_hi = _rwkv_post(yT, bonus, g, _pair_table(rwkv_ln_g[l], B), _pair_table(rwkv_ln_b[l], B),
                                  Tb=T["post_tb"])

        y_na = _neighbourhood_attention(p_na, na_rpb[l])

        half_tiles = lambda a: a.reshape(D // LANES, B * S // 2, LANES)
        x2 = _merge(x2, xn, zc.reshape(B * S, D), half_tiles(yr_lo), half_tiles(yr_hi), y_na.reshape(B * S, D),
                    w_in_bf[:, off_gate:], gate_b[l][None], conv_proj_w[l].astype(BF16), conv_proj_b[l][None],
                    rwkv_proj_w[l].astype(BF16), na_proj_w[l].astype(BF16), w_out[l].astype(BF16),
                    tm=T["merge_tm"], seq=S)

        kv = _norm_matmul(mem2, norm_mem_g[l][None], xa_wkv[l].astype(BF16), jnp.zeros((1, 2 * D), F32),
                          tm=min(512, B * Mm), tn=1024, out_dtype=BF16, name="proj_mem_kv")
        xa = _cross_attention(x2.reshape(B, S, D), norm_xa_g[l][None], xa_wq[l].astype(BF16),
                              kv.reshape(B, Mm, 2 * D), xa_wo[l].astype(BF16), tm=T["xa_tm"])

        x2 = _mlp(xa.reshape(B * S, D), norm_mlp_g[l][None], mlp_w1[l].astype(BF16), mlp_w2[l].astype(BF16),
                  norm_f_g[None], tm=T["mlp_tm"], tf=T["mlp_tf"], final_norm=(l == depth - 1))
    return x2.reshape(B, S, D)
```

```python
import functools
import math

import jax
import jax.numpy as jnp
from jax import lax
from jax.experimental import pallas as pl
from jax.experimental.pallas import tpu as pltpu

F32 = jnp.float32
BF16 = jnp.bfloat16

NORM_EPS = 1e-6
LN_EPS = 1e-5
GRID_W = 64
CONV_W = 31
RWKV_HEAD = 64
R_DECAY = 64
R_ICL = 64
R_GATE = 128
RWKV_GN_EPS = 1e-5 * RWKV_HEAD
NA_HEAD = 64
NA_WIN_R_MAX = 8
NA_WIN_C = 16
XA_HEADS = 4
LANES = 128
VMEM_LIMIT = 48 * 1024 * 1024
EXP_NEG_HALF = math.exp(-0.5)


def _sig(x):
    return 1.0 / (1.0 + jnp.exp(-x))


def _rms(x, g):
    ms = jnp.mean(x * x, axis=-1, keepdims=True)
    return x * lax.rsqrt(ms + NORM_EPS) * g


def _params(sem):
    return pltpu.CompilerParams(dimension_semantics=sem, vmem_limit_bytes=VMEM_LIMIT)


def _resident(whole):
    return {"pipeline_mode": pl.Buffered(1)} if whole else {}


def _norm_matmul_kernel(x_ref, g_ref, w_ref, b_ref, o_ref, xn_ref):
    @pl.when(pl.program_id(1) == 0)
    def _():
        xn_ref[...] = _rms(x_ref[...], g_ref[...]).astype(BF16)

    y = jnp.dot(xn_ref[...], w_ref[...], preferred_element_type=F32) + b_ref[...]
    o_ref[...] = y.astype(o_ref.dtype)


def _norm_matmul(x2, g, w_bf, bias, *, tm, tn, out_dtype=F32, emit_xn=False, name):
    M, K = x2.shape
    N = w_bf.shape[1]
    out_specs = pl.BlockSpec((tm, tn), lambda i, j: (i, j))
    out_shape = jax.ShapeDtypeStruct((M, N), out_dtype)
    scratch = [pltpu.VMEM((tm, K), BF16)]
    if emit_xn:
        out_specs = [out_specs, pl.BlockSpec((tm, K), lambda i, j: (i, 0))]
        out_shape = [out_shape, jax.ShapeDtypeStruct((M, K), BF16)]
        scratch = []
    return pl.pallas_call(
        _norm_matmul_kernel,
        grid=(M // tm, N // tn),
        in_specs=[pl.BlockSpec((tm, K), lambda i, j: (i, 0)),
                  pl.BlockSpec((1, K), lambda i, j: (0, 0)),
                  pl.BlockSpec((K, tn), lambda i, j: (0, j), **_resident(tn == N)),
                  pl.BlockSpec((1, tn), lambda i, j: (0, j))],
        out_specs=out_specs,
        out_shape=out_shape,
        scratch_shapes=scratch,
        compiler_params=_params(("arbitrary", "arbitrary")),
        name=name,
    )(x2, g, w_bf, bias)


def _matmul_kernel(x_ref, w_ref, b_ref, o_ref, *, act):
    y = jnp.dot(x_ref[...], w_ref[...], preferred_element_type=F32) + b_ref[...]
    if act == "sigmoid":
        y = _sig(y)
    o_ref[...] = y.astype(o_ref.dtype)


def _matmul(xn, w_bf, bias, *, tm, tn, act=None, out_dtype=F32, name):
    M, K = xn.shape
    N = w_bf.shape[1]
    return pl.pallas_call(
        functools.partial(_matmul_kernel, act=act),
        grid=(M // tm, N // tn),
        in_specs=[pl.BlockSpec((tm, K), lambda i, j: (i, 0)),
                  pl.BlockSpec((K, tn), lambda i, j: (0, j)),
                  pl.BlockSpec((1, tn), lambda i, j: (0, j))],
        out_specs=pl.BlockSpec((tm, tn), lambda i, j: (i, j)),
        out_shape=jax.ShapeDtypeStruct((M, N), out_dtype),
        compiler_params=_params(("arbitrary", "arbitrary")),
        name=name,
    )(xn, w_bf, bias)


CONV_HALO = 16
CONV_ROWS = 128
CONV_LANES = 128


def _conv_kernel(cur_ref, prev_ref, next_ref, bglu_ref, dww_ref, dwb_ref, lng_ref, lnb_ref, o_ref, zp_ref, cz_ref,
                 *, ts, C):
    i = pl.program_id(1)
    n = pl.num_programs(1)
    H = CONV_HALO
    bg = bglu_ref[...]

    def glu(u):
        return (u[:, :C] + bg[:, :C]) * _sig(u[:, C:] + bg[:, C:])

    zp_ref[H:H + ts, :] = glu(cur_ref[0])
    zp_ref[0:H, :] = jnp.where(i > 0, glu(prev_ref[0]), 0.0)
    zp_ref[H + ts:H + ts + H, :] = jnp.where(i < n - 1, glu(next_ref[0]), 0.0)

    R, LC = CONV_ROWS, CONV_LANES
    first = H - CONV_W // 2

    def conv_body(rc, carry):
        r0 = pl.multiple_of(rc * R, R)
        for lc in range(C // LC):
            ls = slice(lc * LC, (lc + 1) * LC)
            win = zp_ref[pl.ds(r0, R + 2 * H), ls]
            acc = jnp.zeros((R, LC), F32) + dwb_ref[:, ls]
            for rho in range(8):
                part = None
                for k in range(CONV_W):
                    if (first + k) % 8 != rho:
                        continue
                    term = win[first + k:first + k + R, :] * dww_ref[k:k + 1, ls]
                    part = term if part is None else part + term
                if part is not None:
                    acc = acc + part
            cz_ref[pl.ds(r0, R), ls] = acc
        return carry

    lax.fori_loop(0, ts // R, conv_body, 0)

    def ln_body(rc, carry):
        r0 = pl.multiple_of(rc * R, R)
        z = cz_ref[pl.ds(r0, R), :]
        mu = jnp.mean(z, axis=-1, keepdims=True)
        d = z - mu
        var = jnp.mean(d * d, axis=-1, keepdims=True)
        y = d * lax.rsqrt(var + LN_EPS) * lng_ref[...] + lnb_ref[...]
        o_ref[0, pl.ds(r0, R), :] = (y * _sig(y)).astype(o_ref.dtype)
        return carry

    lax.fori_loop(0, ts // R, ln_body, 0, unroll=2)


def _conformer_conv(u, b_glu, dw_w, dw_b, ln_g, ln_b, *, ts):
    B, S, C2 = u.shape
    C = C2 // 2
    H = CONV_HALO
    nh = ts // H
    last = S // H - 1
    dww = jnp.zeros((32, C), F32).at[:CONV_W].set(dw_w)
    return pl.pallas_call(
        functools.partial(_conv_kernel, ts=ts, C=C),
        grid=(B, S // ts),
        in_specs=[pl.BlockSpec((1, ts, C2), lambda b, i: (b, i, 0)),
                  pl.BlockSpec((1, H, C2), lambda b, i: (b, jnp.maximum(i * nh - 1, 0), 0)),
                  pl.BlockSpec((1, H, C2), lambda b, i: (b, jnp.minimum((i + 1) * nh, last), 0)),
                  pl.BlockSpec((1, C2), lambda b, i: (0, 0)),
                  pl.BlockSpec((32, C), lambda b, i: (0, 0)),
                  pl.BlockSpec((1, C), lambda b, i: (0, 0)),
                  pl.BlockSpec((1, C), lambda b, i: (0, 0)),
                  pl.BlockSpec((1, C), lambda b, i: (0, 0))],
        out_specs=pl.BlockSpec((1, ts, C), lambda b, i: (b, i, 0)),
        out_shape=jax.ShapeDtypeStruct((B, S, C), BF16),
        scratch_shapes=[pltpu.VMEM((ts + 2 * H, C), F32), pltpu.VMEM((ts, C), F32)],
        compiler_params=_params(("arbitrary", "arbitrary")),
        name="conformer_conv",
    )(u, u, u, b_glu[None], dww, dw_b[None], ln_g[None], ln_b[None])


def _rwkv_nat_kernel(cur_ref, prev_ref, next_ref, mup_ref, mun_ref, w0_ref, w2_ref, a0_ref, a2_ref, g2_ref,
                     ka_ref, rk_ref, bd_ref,
                     r_o, k_o, v_o, d0_o, d1_o, s0_o, s1_o, g_o, bon_o, *, Tb, D):
    i = pl.program_id(1)
    n = pl.num_programs(1)
    rows = lax.broadcasted_iota(jnp.int32, (Tb, LANES), 0)
    first_row = rows == 0
    last_row = rows == Tb - 1
    has_prev = i > 0
    has_next = i < n - 1

    def shifted(c0, c1):
        outs = []
        for c in range(c0, c1, LANES):
            cs = slice(c, c + LANES)
            x = cur_ref[0, :, cs]
            prow = jnp.where(has_prev, prev_ref[0, 7:8, cs], 0.0)
            nrow = jnp.where(has_next, next_ref[0, 0:1, cs], 0.0)
            xp = jnp.where(first_row, prow, pltpu.roll(x, 1, 0))
            xn = jnp.where(last_row, nrow, pltpu.roll(x, Tb - 1, 0))
            outs.append(x + mup_ref[:, cs] * (xp - x) + mun_ref[:, cs] * (xn - x))
        return outs[0] if len(outs) == 1 else jnp.concatenate(outs, axis=-1)

    r = shifted(0, D)
    k = shifted(D, 2 * D)
    v = shifted(2 * D, 3 * D)
    wd = shifted(3 * D, 3 * D + 2 * R_DECAY)
    ad = shifted(3 * D + 2 * R_DECAY, 3 * D + 2 * R_DECAY + 2 * R_ICL)
    gd = shifted(3 * D + 2 * R_DECAY + 2 * R_ICL, 3 * D + 2 * R_DECAY + 2 * R_ICL + R_GATE)

    zw = w0_ref[...] + jnp.dot(jnp.tanh(wd).astype(BF16), w2_ref[...], preferred_element_type=F32)
    dec = jnp.exp(-EXP_NEG_HALF * _sig(zw))
    asig = _sig(a0_ref[...] + jnp.dot(ad.astype(BF16), a2_ref[...], preferred_element_type=F32))
    g = jnp.dot(_sig(gd).astype(BF16), g2_ref[...], preferred_element_type=F32)

    s0 = asig[:, :D]
    s1 = asig[:, D:]
    e = r * k * (2.0 + (s0 + s1 - 2.0) * ka_ref[...]) * rk_ref[...]
    parts = []
    for c in range(0, D, LANES):
        et = e[:, c:c + LANES]
        hi = et.astype(BF16)
        lo = (et - hi.astype(F32)).astype(BF16)
        parts.append(jnp.dot(hi, bd_ref[...], preferred_element_type=F32)
                     + jnp.dot(lo, bd_ref[...], preferred_element_type=F32))
    rk_b = jnp.concatenate(parts, axis=-1)

    def store_tiles(o_ref, val):
        for q in range(D // LANES):
            o_ref[q, 0] = val[:, q * LANES:(q + 1) * LANES]

    store_tiles(r_o, r)
    store_tiles(k_o, k)
    store_tiles(v_o, v)
    store_tiles(d0_o, dec[:, :D])
    store_tiles(d1_o, dec[:, D:])
    store_tiles(s0_o, s0)
    store_tiles(s1_o, s1)
    store_tiles(g_o, g)
    store_tiles(bon_o, rk_b * v)


def _rwkv_nat(p, mu_prev, mu_next, w0, w2, a0, a2, g2, k_a, r_k, *, Tb):
    B, S, NR = p.shape
    D = (NR - 2 * R_DECAY - 2 * R_ICL - R_GATE) // 3
    nh = Tb // 8
    last = S // 8 - 1
    zero = jnp.zeros((R_DECAY, D), F32)
    w2bd = jnp.concatenate([jnp.concatenate([w2[0], zero], 1), jnp.concatenate([zero, w2[1]], 1)], 0).astype(BF16)
    a2bd = jnp.concatenate([jnp.concatenate([a2[0], zero], 1), jnp.concatenate([zero, a2[1]], 1)], 0).astype(BF16)
    half = jnp.arange(LANES) // RWKV_HEAD
    bd = (half[:, None] == half[None, :]).astype(BF16)
    full = lambda shape: pl.BlockSpec(shape, lambda b, i: (0,) * len(shape))
    out_spec = pl.BlockSpec((D // LANES, 1, Tb, LANES), lambda b, i: (0, b, i, 0))
    out = jax.ShapeDtypeStruct((D // LANES, B, S, LANES), F32)
    return pl.pallas_call(
        functools.partial(_rwkv_nat_kernel, Tb=Tb, D=D),
        grid=(B, S // Tb),
        in_specs=[pl.BlockSpec((1, Tb, NR), lambda b, i: (b, i, 0)),
                  pl.BlockSpec((1, 8, NR), lambda b, i: (b, jnp.maximum(i * nh - 1, 0), 0)),
                  pl.BlockSpec((1, 8, NR), lambda b, i: (b, jnp.minimum((i + 1) * nh, last), 0)),
                  full((1, NR)), full((1, NR)),
                  full((1, 2 * D)), full((2 * R_DECAY, 2 * D)),
                  full((1, 2 * D)), full((2 * R_ICL, 2 * D)),
                  full((R_GATE, D)), full((1, D)), full((1, D)), full((LANES, LANES))],
        out_specs=[out_spec] * 9,
        out_shape=[out] * 9,
        compiler_params=_params(("arbitrary", "arbitrary")),
        name="rwkv_tokens",
    )(p, p, p, mu_prev[None], mu_next[None], w0.reshape(1, 2 * D), w2bd, a0.reshape(1, 2 * D), a2bd,
      g2.astype(BF16), k_a[None], r_k.reshape(1, D), bd)


SCAN_PLANES = 32
SCAN_ACC = 2
SCAN_GROUP = 4
SCAN_ROW_PAD = 8
OP_R, OP_K, OP_V, OP_A, OP_B = range(5)


def _scan_kernel(rf, rm, kf, km, vf, vm, df, dm, sf, sm_, kk_ref, ka_ref, y_ref, P_ref, ops_ref, in_ref, c_ref,
                 *, Tc):
    @pl.when(pl.program_id(0) == 0)
    def _():
        P_ref[...] = jnp.zeros(P_ref.shape, F32)

    c_ref[...] = jnp.ones(c_ref.shape, F32)

    HD = RWKV_HEAD
    half = LANES // 2
    zeros = jnp.zeros((HD, LANES), F32)
    kkT = kk_ref[...]
    kaT = ka_ref[...]
    srcs = (rf, rm, kf, km, vf, vm, df, dm, sf, sm_)
    RF, RM, KF, KM, VF, VM, DF, DM, SF, SM = range(len(srcs))
    RS = Tc + SCAN_ROW_PAD
    for qi, src in enumerate(srcs):
        for n in range(half):
            in_ref[qi, n * RS:n * RS + Tc, :] = src[n // src.shape[1], n % src.shape[1]]

    def stacked(f, m, s):
        top = in_ref[f, pl.ds(s, half, stride=RS), :]
        bot = in_ref[m, pl.ds(Tc - 1 - s, half, stride=RS), :]
        return jnp.concatenate([top, bot], axis=0).T

    def build(s, slot):
        kT = stacked(KF, KM, s)
        sT = stacked(SF, SM, s)
        kk = (kT * kkT).reshape(2, HD, LANES)
        nrm = jnp.sqrt(jnp.sum(kk * kk, axis=1, keepdims=True))
        kk = (kk / jnp.maximum(nrm, 1e-12)).reshape(LANES, LANES)
        c_prev = c_ref[...]
        c = c_prev * stacked(DF, DM, s)
        c_ref[...] = c
        c_inv = 1.0 / c
        ops_ref[slot, OP_R] = stacked(RF, RM, s) * c
        ops_ref[slot, OP_K] = kT * (1.0 + (sT - 1.0) * kaT) * c_inv
        ops_ref[slot, OP_V] = stacked(VF, VM, s)
        ops_ref[slot, OP_A] = -kk * c_prev
        ops_ref[slot, OP_B] = kk * sT * c_inv

    def sweep(s, slot, unscale):
        def op_row(idx, row):
            return ops_ref[slot, idx, pl.ds(row, 1), :]

        for g in range(LANES // HD):
            def sa_body(jb, sas):
                sas = list(sas)
                for jj in range(SCAN_PLANES):
                    row = g * HD + jb * SCAN_PLANES + jj
                    sas[jj % SCAN_ACC] = sas[jj % SCAN_ACC] + P_ref[row] * op_row(OP_A, row)
                return tuple(sas)

            sa = sum(lax.fori_loop(0, HD // SCAN_PLANES, sa_body, (zeros,) * SCAN_ACC))
            v = ops_ref[slot, OP_V, g * HD:(g + 1) * HD, :]

            def upd_body(jb, ys):
                ys = list(ys)
                for jj in range(SCAN_PLANES):
                    row = g * HD + jb * SCAN_PLANES + jj
                    pn = (P_ref[row] + sa * op_row(OP_B, row)) + v * op_row(OP_K, row)
                    ys[jj % SCAN_ACC] = ys[jj % SCAN_ACC] + pn * op_row(OP_R, row)
                    P_ref[row] = pn * c_ref[pl.ds(row, 1), :] if unscale else pn
                return tuple(ys)

            y_ref[s, g * HD:(g + 1) * HD, :] = sum(lax.fori_loop(0, HD // SCAN_PLANES, upd_body, (zeros,) * SCAN_ACC))

    def step_group(sg, last):
        s0 = sg * SCAN_GROUP
        for u in range(SCAN_GROUP):
            build(s0 + u, u)
        for u in range(SCAN_GROUP):
            sweep(s0 + u, u, last and u == SCAN_GROUP - 1)

    groups = Tc // SCAN_GROUP

    def group_body(sg, carry):
        step_group(sg, False)
        return carry

    lax.fori_loop(0, groups - 1, group_body, 0)
    step_group(groups - 1, True)


def _rwkv_scan(r, k, v, d0, d1, s0, s1, k_kT, k_aT, *, Tc):
    P, B, S, _ = r.shape
    nC = S // Tc
    fwd = pl.BlockSpec((P, B, Tc, LANES), lambda c: (0, 0, c, 0))
    mir = pl.BlockSpec((P, B, Tc, LANES), lambda c: (0, 0, nC - 1 - c, 0))
    tab = pl.BlockSpec((LANES, LANES), lambda c: (0, 0))
    return pl.pallas_call(
        functools.partial(_scan_kernel, Tc=Tc),
        grid=(nC,),
        in_specs=[fwd, mir, fwd, mir, fwd, mir, fwd, mir, fwd, mir, tab, tab],
        out_specs=pl.BlockSpec((Tc, LANES, LANES), lambda c: (c, 0, 0)),
        out_shape=jax.ShapeDtypeStruct((S, LANES, LANES), F32),
        scratch_shapes=[pltpu.VMEM((LANES, RWKV_HEAD, LANES), F32), pltpu.VMEM((SCAN_GROUP, 5, LANES, LANES), F32),
                        pltpu.VMEM((10, P * B * (Tc + SCAN_ROW_PAD), LANES), F32), pltpu.VMEM((LANES, LANES), F32)],
        compiler_params=_params(("arbitrary",)),
        name="rwkv_scan",
    )(r, r, k, k, v, v, d0, d1, s0, s1, k_kT, k_aT)


def _rwkv_post_kernel(yf_ref, ym_ref, bonf_ref, bonm_ref, gf_ref, gm_ref, lng_ref, lnb_ref, of_ref, om_ref, tok_ref,
                      *, Tb):
    half = LANES // 2

    def body(s, carry):
        y = yf_ref[s] + pltpu.roll(ym_ref[Tb - 1 - s], half, 1)
        y = y.reshape(2, RWKV_HEAD, LANES)
        mu = jnp.mean(y, axis=1, keepdims=True)
        d = y - mu
        var = jnp.mean(d * d, axis=1, keepdims=True)
        yn = (d * lax.rsqrt(var + RWKV_GN_EPS)).reshape(LANES, LANES) * lng_ref[...] + lnb_ref[...]
        tok = yn.T
        tok_ref[0, pl.ds(s, half, stride=Tb), :] = tok[:half]
        tok_ref[1, pl.ds(Tb - 1 - s, half, stride=Tb), :] = tok[half:]
        return carry

    lax.fori_loop(0, Tb, body, 0, unroll=2)
    nb = of_ref.shape[1]
    for n in range(half):
        p, b = n // nb, n % nb
        rows = slice(n * Tb, (n + 1) * Tb)
        of_ref[p, b] = (tok_ref[0, rows, :] + bonf_ref[p, b]) * gf_ref[p, b]
        om_ref[p, b] = (tok_ref[1, rows, :] + bonm_ref[p, b]) * gm_ref[p, b]


def _rwkv_post(yT, bonus, g, ln_gT, ln_bT, *, Tb):
    P, B, S, _ = bonus.shape
    nC = S // Tb
    nH = nC // 2
    fwd = pl.BlockSpec((P, B, Tb, LANES), lambda c: (0, 0, c, 0))
    mir = pl.BlockSpec((P, B, Tb, LANES), lambda c: (0, 0, nC - 1 - c, 0))
    tab = pl.BlockSpec((LANES, LANES), lambda c: (0, 0))
    out = jax.ShapeDtypeStruct((P, B, S // 2, LANES), F32)
    return pl.pallas_call(
        functools.partial(_rwkv_post_kernel, Tb=Tb),
        grid=(nH,),
        in_specs=[pl.BlockSpec((Tb, LANES, LANES), lambda c: (c, 0, 0)),
                  pl.BlockSpec((Tb, LANES, LANES), lambda c: (nC - 1 - c, 0, 0)),
                  fwd, mir, fwd, mir, tab, tab],
        out_specs=[fwd, pl.BlockSpec((P, B, Tb, LANES), lambda c: (0, 0, nH - 1 - c, 0))],
        out_shape=[out, out],
        scratch_shapes=[pltpu.VMEM((2, P * B * Tb, LANES), F32)],
        compiler_params=_params(("arbitrary",)),
        name="rwkv_groupnorm",
    )(yT, yT, bonus, bonus, g, g, ln_gT, ln_bT)


def _pair_table(p, batch):
    pairs = p.shape[0] // LANES
    t = jnp.repeat(p.reshape(pairs, LANES).T, batch, axis=1)
    return jnp.tile(t, (1, LANES // (pairs * batch)))


NA_GROUP = 4


def _na_kernel(q_ref, k_ref, v_ref, bias_ref, o_ref, *, rows, win):
    GW = GRID_W
    lane = lax.broadcasted_iota(jnp.int32, (2 * GW, LANES), 1)
    head = lax.broadcasted_iota(jnp.int32, (2 * GW, LANES), 0) // GW
    own = (lane // NA_HEAD) == head
    qc = lax.broadcasted_iota(jnp.int32, (2 * GW, win * GW), 0) % GW
    kc = lax.broadcasted_iota(jnp.int32, (2 * GW, win * GW), 1) % GW
    cs = jnp.clip(qc - NA_WIN_C // 2, 0, GW - NA_WIN_C)
    ok = (kc >= cs) & (kc < cs + NA_WIN_C)
    low = lax.broadcasted_iota(jnp.int32, (GW, LANES), 1) < NA_HEAD
    scale = NA_HEAD ** -0.5

    def body(rg, carry):
        scores, vbs, qss = [], [], []
        for i in range(NA_GROUP):
            r = rg * NA_GROUP + i
            r0 = jnp.clip(r - win // 2, 0, rows - win)
            qs = pl.multiple_of(r * GW, GW)
            ks = pl.multiple_of(r0 * GW, GW)
            q = q_ref[0, pl.ds(qs, GW), :] * scale
            qq = jnp.where(own, jnp.concatenate([q, q], axis=0), 0.0).astype(BF16)
            kb = k_ref[0, pl.ds(ks, win * GW), :]
            s = lax.dot_general(qq, kb, (((1,), (1,)), ((), ())), preferred_element_type=F32)
            scores.append(jnp.where(ok, s + bias_ref[0, r - r0], -1e30))
            vbs.append(v_ref[0, pl.ds(ks, win * GW), :])
            qss.append(qs)
        probs, sums = [], []
        for s in scores:
            p = jnp.exp(s - jnp.max(s, axis=-1, keepdims=True))
            sums.append(jnp.sum(p, axis=-1, keepdims=True))
            probs.append(p.astype(BF16))
        for p, l, vb, qs in zip(probs, sums, vbs, qss):
            o = jnp.dot(p, vb, preferred_element_type=F32) / l
            o_ref[0, pl.ds(qs, GW), :] = jnp.where(low, o[:GW], o[GW:]).astype(o_ref.dtype)
        return carry

    lax.fori_loop(0, rows // NA_GROUP, body, 0)


def _na_bias(rpb, win):
    col = jnp.arange(GRID_W)
    dx = jnp.clip(col[None, :] - col[:, None], 1 - NA_WIN_C, NA_WIN_C - 1) + NA_WIN_C - 1
    d = jnp.arange(win)
    p = jnp.arange(win)
    dy = p[None, :] - d[:, None] + NA_WIN_R_MAX - 1
    t = rpb[:, dy][:, :, :, dx]
    H = rpb.shape[0]
    t = t.reshape(H // 2, 2, win, win, GRID_W, GRID_W).transpose(0, 2, 1, 4, 3, 5)
    return t.reshape(H // 2, win, 2 * GRID_W, win * GRID_W).astype(F32)


def _neighbourhood_attention(p, rpb):
    B, S, D3 = p.shape
    D = D3 // 3
    pairs = D // LANES
    rows = S // GRID_W
    win = min(NA_WIN_R_MAX, rows)
    bias = _na_bias(rpb, win)
    return pl.pallas_call(
        functools.partial(_na_kernel, rows=rows, win=win),
        grid=(B, pairs),
        in_specs=[pl.BlockSpec((1, S, LANES), lambda b, h: (b, 0, h)),
                  pl.BlockSpec((1, S, LANES), lambda b, h: (b, 0, pairs + h)),
                  pl.BlockSpec((1, S, LANES), lambda b, h: (b, 0, 2 * pairs + h)),
                  pl.BlockSpec((1, win, 2 * GRID_W, win * GRID_W), lambda b, h: (h, 0, 0, 0))],
        out_specs=pl.BlockSpec((1, S, LANES), lambda b, h: (b, 0, h)),
        out_shape=jax.ShapeDtypeStruct((B, S, D), BF16),
        compiler_params=_params(("arbitrary", "arbitrary")),
        name="neighbourhood_attention",
    )(p, p, p, bias)


def _merge_kernel(x_ref, xn_ref, zc_ref, yr_lo_ref, yr_hi_ref, yn_ref, wg_ref, bg_ref, wc_ref, bc_ref, wr_ref, wn_ref,
                  wo_ref, o_ref, *, seq):
    tm, D = x_ref.shape
    gate = lambda k: _sig(jnp.dot(xn_ref[...], wg_ref[:, k * D:(k + 1) * D], preferred_element_type=F32)
                          + bg_ref[:, k * D:(k + 1) * D])
    yc = jnp.dot(zc_ref[...].astype(BF16), wc_ref[...], preferred_element_type=F32) + bc_ref[...]
    second_half = (pl.program_id(0) * tm) % seq >= seq // 2
    tiles = lambda ref: jnp.concatenate([ref[q] for q in range(ref.shape[0])], axis=-1)
    yr_in = jnp.where(second_half, tiles(yr_hi_ref), tiles(yr_lo_ref))
    yr = jnp.dot(yr_in.astype(BF16), wr_ref[...], preferred_element_type=F32)
    yn = jnp.dot(yn_ref[...].astype(BF16), wn_ref[...], preferred_element_type=F32)
    merged = gate(0) * yc + gate(1) * yr + gate(2) * yn
    o_ref[...] = x_ref[...] + jnp.dot(merged.astype(BF16), wo_ref[...], preferred_element_type=F32)


def _merge(x2, xn, zc, yr_lo, yr_hi, yn, wg, bg, wc, bc, wr, wn, wo, *, tm, seq):
    M, D = x2.shape
    hs = seq // 2
    row = pl.BlockSpec((tm, D), lambda i: (i, 0))
    wsp = pl.BlockSpec((D, D), lambda i: (0, 0), pipeline_mode=pl.Buffered(1))
    half_rows = pl.BlockSpec((D // LANES, tm, LANES), lambda i: (0, (((i * tm) // seq) * hs + (i * tm) % hs) // tm, 0))
    return pl.pallas_call(
        functools.partial(_merge_kernel, seq=seq),
        grid=(M // tm,),
        in_specs=[row, row, row, half_rows, half_rows, row,
                  pl.BlockSpec((D, 3 * D), lambda i: (0, 0), pipeline_mode=pl.Buffered(1)),
                  pl.BlockSpec((1, 3 * D), lambda i: (0, 0)),
                  wsp, pl.BlockSpec((1, D), lambda i: (0, 0)), wsp, wsp, wsp],
        out_specs=row,
        out_shape=jax.ShapeDtypeStruct((M, D), F32),
        compiler_params=_params(("arbitrary",)),
        name="gated_merge",
    )(x2, xn, zc, yr_lo, yr_hi, yn, wg, bg, wc, bc, wr, wn, wo)


def _xattn_kernel(x_ref, g_ref, wq_ref, kv_ref, wo_ref, o_ref):
    x = x_ref[0]
    D = x.shape[-1]
    dh = D // XA_HEADS
    h = _rms(x, g_ref[...]).astype(BF16)
    q = jnp.dot(h, wq_ref[...], preferred_element_type=F32)
    outs = []
    for hd in range(XA_HEADS):
        qh = q[:, hd * dh:(hd + 1) * dh].astype(BF16)
        km = kv_ref[0, :, hd * dh:(hd + 1) * dh]
        vm = kv_ref[0, :, D + hd * dh:D + (hd + 1) * dh]
        s = lax.dot_general(qh, km, (((1,), (1,)), ((), ())), preferred_element_type=F32) * dh ** -0.5
        m = jnp.max(s, axis=-1, keepdims=True)
        p = jnp.exp(s - m)
        l = jnp.sum(p, axis=-1, keepdims=True)
        outs.append(jnp.dot(p.astype(BF16), vm, preferred_element_type=F32) / l)
    o = jnp.concatenate(outs, axis=-1).astype(BF16)
    o_ref[0] = x + jnp.dot(o, wo_ref[...], preferred_element_type=F32)


def _cross_attention(x, g, wq, kv, wo, *, tm):
    B, S, D = x.shape
    Mm = kv.shape[1]
    row = pl.BlockSpec((1, tm, D), lambda b, i: (b, i, 0))
    wsp = pl.BlockSpec((D, D), lambda b, i: (0, 0))
    return pl.pallas_call(
        _xattn_kernel,
        grid=(B, S // tm),
        in_specs=[row, pl.BlockSpec((1, D), lambda b, i: (0, 0)), wsp,
                  pl.BlockSpec((1, Mm, 2 * D), lambda b, i: (b, 0, 0)), wsp],
        out_specs=row,
        out_shape=jax.ShapeDtypeStruct((B, S, D), F32),
        compiler_params=_params(("arbitrary", "arbitrary")),
        name="memory_cross_attention",
    )(x, g, wq, kv, wo)


def _mlp_kernel(x_ref, g_ref, w1_ref, w2_ref, gf_ref, o_ref, xn_ref, acc_ref, *, final_norm):
    f = pl.program_id(1)

    @pl.when(f == 0)
    def _():
        xn_ref[...] = _rms(x_ref[...], g_ref[...]).astype(BF16)
        acc_ref[...] = jnp.zeros(acc_ref.shape, F32)

    h = jnp.maximum(jnp.dot(xn_ref[...], w1_ref[...], preferred_element_type=F32), 0.0)
    acc_ref[...] += jnp.dot((h * h).astype(BF16), w2_ref[...], preferred_element_type=F32)

    @pl.when(f == pl.num_programs(1) - 1)
    def _():
        y = x_ref[...] + acc_ref[...]
        if final_norm:
            y = _rms(y, gf_ref[...])
        o_ref[...] = y


def _mlp(x2, g, w1, w2, gf, *, tm, tf, final_norm):
    M, D = x2.shape
    F = w1.shape[1]
    row = pl.BlockSpec((tm, D), lambda i, f: (i, 0))
    vec = pl.BlockSpec((1, D), lambda i, f: (0, 0))
    return pl.pallas_call(
        functools.partial(_mlp_kernel, final_norm=final_norm),
        grid=(M // tm, F // tf),
        in_specs=[row, vec, pl.BlockSpec((D, tf), lambda i, f: (0, f), **_resident(tf == F)),
                  pl.BlockSpec((tf, D), lambda i, f: (f, 0), **_resident(tf == F)), vec],
        out_specs=row,
        out_shape=jax.ShapeDtypeStruct((M, D), F32),
        scratch_shapes=[pltpu.VMEM((tm, D), BF16), pltpu.VMEM((tm, D), F32)],
        compiler_params=_params(("arbitrary", "arbitrary")),
        name="relu2_mlp",
    )(x2, g, w1, w2, gf)


def _tiles(S):
    return dict(
        proj_tm=512,
        side_tm=2048,
        side_tn=1024,
        conv_ts=min(512, S),
        nat_tb=min(128, S),
        post_tb=16,
        scan_tc=16,
        merge_tm=min(512, S // 2),
        xa_tm=min(512, S),
        mlp_tm=512,
    )


def kernel(x, mem, norm_mix_g, w_in, gate_b, conv_b_glu, conv_dw_w, conv_dw_b, conv_ln_g, conv_ln_b,
           conv_proj_w, conv_proj_b, rwkv_mu_prev, rwkv_mu_next, rwkv_w0, rwkv_w2, rwkv_a0, rwkv_a2,
           rwkv_g2, rwkv_k_k, rwkv_k_a, rwkv_r_k, rwkv_ln_g, rwkv_ln_b, rwkv_proj_w, na_rpb, na_proj_w,
           w_out, norm_xa_g, norm_mem_g, xa_wq, xa_wkv, xa_wo, norm_mlp_g, mlp_w1, mlp_w2, norm_f_g):
    B, S, D = x.shape
    depth = w_in.shape[0]
    Mm = mem.shape[1]
    T = _tiles(S)
    n_conv = 2 * D
    n_rwkv = 3 * D + 2 * R_DECAY + 2 * R_ICL + R_GATE
    n_na = 3 * D
    off_rwkv = n_conv
    off_na = off_rwkv + n_rwkv
    off_gate = off_na + n_na

    x2 = x.reshape(B * S, D)
    mem2 = mem.reshape(B * Mm, D)
    for l in range(depth):
        w_in_bf = w_in[l].astype(BF16)
        g_mix = norm_mix_g[l][None]

        p_rwkv, xn = _norm_matmul(x2, g_mix, w_in_bf[:, off_rwkv:off_na], jnp.zeros((1, n_rwkv), F32),
                                  tm=T["proj_tm"], tn=n_rwkv, emit_xn=True, name="proj_rwkv")
        p_rwkv = p_rwkv.reshape(B, S, n_rwkv)

        def proj(c0, c1, name, bias=None, act=None, out_dtype=F32):
            n = c1 - c0
            b = jnp.zeros((1, n), F32) if bias is None else bias[None]
            return _matmul(xn, w_in_bf[:, c0:c1], b, tm=T["side_tm"], tn=T["side_tn"], act=act,
                           out_dtype=out_dtype, name=name)

        p_conv = proj(0, off_rwkv, "proj_conv").reshape(B, S, n_conv)
        p_na = proj(off_na, off_gate, "proj_na", out_dtype=BF16).reshape(B, S, n_na)

        zc = _conformer_conv(p_conv, conv_b_glu[l], conv_dw_w[l], conv_dw_b[l], conv_ln_g[l], conv_ln_b[l],
                             ts=T["conv_ts"])

        r, k, v, d0, d1, s0, s1, g, bonus = _rwkv_nat(
            p_rwkv, rwkv_mu_prev[l], rwkv_mu_next[l], rwkv_w0[l], rwkv_w2[l], rwkv_a0[l], rwkv_a2[l],
            rwkv_g2[l], rwkv_k_a[l], rwkv_r_k[l], Tb=T["nat_tb"])
        yT = _rwkv_scan(r, k, v, d0, d1, s0, s1, _pair_table(rwkv_k_k[l], B), _pair_table(rwkv_k_a[l], B),
                        Tc=T["scan_tc"])
        yr_lo, yr_hi = _rwkv_post(yT, bonus, g, _pair_table(rwkv_ln_g[l], B), _pair_table(rwkv_ln_b[l], B),
                                  Tb=T["post_tb"])

        y_na = _neighbourhood_attention(p_na, na_rpb[l])

        half_tiles = lambda a: a.reshape(D // LANES, B * S // 2, LANES)
        x2 = _merge(x2, xn, zc.reshape(B * S, D), half_tiles(yr_lo), half_tiles(yr_hi), y_na.reshape(B * S, D),
                    w_in_bf[:, off_gate:], gate_b[l][None], conv_proj_w[l].astype(BF16), conv_proj_b[l][None],
                    rwkv_proj_w[l].astype(BF16), na_proj_w[l].astype(BF16), w_out[l].astype(BF16),
                    tm=T["merge_tm"], seq=S)

        kv = _norm_matmul(mem2, norm_mem_g[l][None], xa_wkv[l].astype(BF16), jnp.zeros((1, 2 * D), F32),
                          tm=min(512, B * Mm), tn=1024, out_dtype=BF16, name="proj_mem_kv")
        xa = _cross_attention(x2.reshape(B, S, D), norm_xa_g[l][None], xa_wq[l].astype(BF16),
                              kv.reshape(B, Mm, 2 * D), xa_wo[l].astype(BF16), tm=T["xa_tm"])

        x2 = _mlp(xa.reshape(B * S, D), norm_mlp_g[l][None], mlp_w1[l].astype(BF16), mlp_w2[l].astype(BF16),
                  norm_f_g[None], tm=T["mlp_tm"], tf=mlp_w1.shape[-1], final_norm=(l == depth - 1))
    return x2.reshape(B, S, D)
```

```python
import functools
import math

import jax
import jax.numpy as jnp
from jax import lax
from jax.experimental import pallas as pl
from jax.experimental.pallas import tpu as pltpu

F32 = jnp.float32
BF16 = jnp.bfloat16

NORM_EPS = 1e-6
LN_EPS = 1e-5
GRID_W = 64
CONV_W = 31
RWKV_HEAD = 64
R_DECAY = 64
R_ICL = 64
R_GATE = 128
RWKV_GN_EPS = 1e-5 * RWKV_HEAD
NA_HEAD = 64
NA_WIN_R_MAX = 8
NA_WIN_C = 16
XA_HEADS = 4
LANES = 128
VMEM_LIMIT = 48 * 1024 * 1024
EXP_NEG_HALF = math.exp(-0.5)


def _sig(x):
    return 1.0 / (1.0 + jnp.exp(-x))


def _rms(x, g):
    ms = jnp.mean(x * x, axis=-1, keepdims=True)
    return x * lax.rsqrt(ms + NORM_EPS) * g


def _params(sem):
    return pltpu.CompilerParams(dimension_semantics=sem, vmem_limit_bytes=VMEM_LIMIT)


def _resident(whole):
    return {"pipeline_mode": pl.Buffered(1)} if whole else {}


def _norm_matmul_kernel(x_ref, g_ref, w_ref, b_ref, o_ref, xn_ref):
    @pl.when(pl.program_id(1) == 0)
    def _():
        xn_ref[...] = _rms(x_ref[...], g_ref[...]).astype(BF16)

    y = jnp.dot(xn_ref[...], w_ref[...], preferred_element_type=F32) + b_ref[...]
    o_ref[...] = y.astype(o_ref.dtype)


def _norm_matmul(x2, g, w_bf, bias, *, tm, tn, out_dtype=F32, emit_xn=False, name):
    M, K = x2.shape
    N = w_bf.shape[1]
    out_specs = pl.BlockSpec((tm, tn), lambda i, j: (i, j))
    out_shape = jax.ShapeDtypeStruct((M, N), out_dtype)
    scratch = [pltpu.VMEM((tm, K), BF16)]
    if emit_xn:
        out_specs = [out_specs, pl.BlockSpec((tm, K), lambda i, j: (i, 0))]
        out_shape = [out_shape, jax.ShapeDtypeStruct((M, K), BF16)]
        scratch = []
    return pl.pallas_call(
        _norm_matmul_kernel,
        grid=(M // tm, N // tn),
        in_specs=[pl.BlockSpec((tm, K), lambda i, j: (i, 0)),
                  pl.BlockSpec((1, K), lambda i, j: (0, 0)),
                  pl.BlockSpec((K, tn), lambda i, j: (0, j), **_resident(tn == N)),
                  pl.BlockSpec((1, tn), lambda i, j: (0, j))],
        out_specs=out_specs,
        out_shape=out_shape,
        scratch_shapes=scratch,
        compiler_params=_params(("arbitrary", "arbitrary")),
        name=name,
    )(x2, g, w_bf, bias)


def _matmul_kernel(x_ref, w_ref, b_ref, o_ref, *, act):
    y = jnp.dot(x_ref[...], w_ref[...], preferred_element_type=F32) + b_ref[...]
    if act == "sigmoid":
        y = _sig(y)
    o_ref[...] = y.astype(o_ref.dtype)


def _matmul(xn, w_bf, bias, *, tm, tn, act=None, out_dtype=F32, name):
    M, K = xn.shape
    N = w_bf.shape[1]
    return pl.pallas_call(
        functools.partial(_matmul_kernel, act=act),
        grid=(M // tm, N // tn),
        in_specs=[pl.BlockSpec((tm, K), lambda i, j: (i, 0)),
                  pl.BlockSpec((K, tn), lambda i, j: (0, j)),
                  pl.BlockSpec((1, tn), lambda i, j: (0, j))],
        out_specs=pl.BlockSpec((tm, tn), lambda i, j: (i, j)),
        out_shape=jax.ShapeDtypeStruct((M, N), out_dtype),
        compiler_params=_params(("arbitrary", "arbitrary")),
        name=name,
    )(xn, w_bf, bias)


CONV_HALO = 16
CONV_ROWS = 128
CONV_LANES = 128


def _conv_kernel(cur_ref, prev_ref, next_ref, bglu_ref, dww_ref, dwb_ref, lng_ref, lnb_ref, o_ref, zp_ref, cz_ref,
                 *, ts, C):
    i = pl.program_id(1)
    n = pl.num_programs(1)
    H = CONV_HALO
    bg = bglu_ref[...]

    def glu(u):
        return (u[:, :C] + bg[:, :C]) * _sig(u[:, C:] + bg[:, C:])

    zp_ref[H:H + ts, :] = glu(cur_ref[0])
    zp_ref[0:H, :] = jnp.where(i > 0, glu(prev_ref[0]), 0.0)
    zp_ref[H + ts:H + ts + H, :] = jnp.where(i < n - 1, glu(next_ref[0]), 0.0)

    R, LC = CONV_ROWS, CONV_LANES
    first = H - CONV_W // 2

    def conv_body(rc, carry):
        r0 = pl.multiple_of(rc * R, R)
        for lc in range(C // LC):
            ls = slice(lc * LC, (lc + 1) * LC)
            win = zp_ref[pl.ds(r0, R + 2 * H), ls]
            acc = jnp.zeros((R, LC), F32) + dwb_ref[:, ls]
            for rho in range(8):
                part = None
                for k in range(CONV_W):
                    if (first + k) % 8 != rho:
                        continue
                    term = win[first + k:first + k + R, :] * dww_ref[k:k + 1, ls]
                    part = term if part is None else part + term
                if part is not None:
                    acc = acc + part
            cz_ref[pl.ds(r0, R), ls] = acc
        return carry

    lax.fori_loop(0, ts // R, conv_body, 0)

    def ln_body(rc, carry):
        r0 = pl.multiple_of(rc * R, R)
        z = cz_ref[pl.ds(r0, R), :]
        mu = jnp.mean(z, axis=-1, keepdims=True)
        d = z - mu
        var = jnp.mean(d * d, axis=-1, keepdims=True)
        y = d * lax.rsqrt(var + LN_EPS) * lng_ref[...] + lnb_ref[...]
        o_ref[0, pl.ds(r0, R), :] = (y * _sig(y)).astype(o_ref.dtype)
        return carry

    lax.fori_loop(0, ts // R, ln_body, 0, unroll=2)


def _conformer_conv(u, b_glu, dw_w, dw_b, ln_g, ln_b, *, ts):
    B, S, C2 = u.shape
    C = C2 // 2
    H = CONV_HALO
    nh = ts // H
    last = S // H - 1
    dww = jnp.zeros((32, C), F32).at[:CONV_W].set(dw_w)
    return pl.pallas_call(
        functools.partial(_conv_kernel, ts=ts, C=C),
        grid=(B, S // ts),
        in_specs=[pl.BlockSpec((1, ts, C2), lambda b, i: (b, i, 0)),
                  pl.BlockSpec((1, H, C2), lambda b, i: (b, jnp.maximum(i * nh - 1, 0), 0)),
                  pl.BlockSpec((1, H, C2), lambda b, i: (b, jnp.minimum((i + 1) * nh, last), 0)),
                  pl.BlockSpec((1, C2), lambda b, i: (0, 0)),
                  pl.BlockSpec((32, C), lambda b, i: (0, 0)),
                  pl.BlockSpec((1, C), lambda b, i: (0, 0)),
                  pl.BlockSpec((1, C), lambda b, i: (0, 0)),
                  pl.BlockSpec((1, C), lambda b, i: (0, 0))],
        out_specs=pl.BlockSpec((1, ts, C), lambda b, i: (b, i, 0)),
        out_shape=jax.ShapeDtypeStruct((B, S, C), BF16),
        scratch_shapes=[pltpu.VMEM((ts + 2 * H, C), F32), pltpu.VMEM((ts, C), F32)],
        compiler_params=_params(("arbitrary", "arbitrary")),
        name="conformer_conv",
    )(u, u, u, b_glu[None], dww, dw_b[None], ln_g[None], ln_b[None])


def _rwkv_nat_kernel(cur_ref, prev_ref, next_ref, mup_ref, mun_ref, w0_ref, w2_ref, a0_ref, a2_ref, g2_ref,
                     ka_ref, rk_ref, bd_ref,
                     r_o, k_o, v_o, d0_o, d1_o, s0_o, s1_o, g_o, bon_o, *, Tb, D):
    i = pl.program_id(1)
    n = pl.num_programs(1)
    rows = lax.broadcasted_iota(jnp.int32, (Tb, LANES), 0)
    first_row = rows == 0
    last_row = rows == Tb - 1
    has_prev = i > 0
    has_next = i < n - 1

    def shifted(c0, c1):
        outs = []
        for c in range(c0, c1, LANES):
            cs = slice(c, c + LANES)
            x = cur_ref[0, :, cs]
            prow = jnp.where(has_prev, prev_ref[0, 7:8, cs], 0.0)
            nrow = jnp.where(has_next, next_ref[0, 0:1, cs], 0.0)
            xp = jnp.where(first_row, prow, pltpu.roll(x, 1, 0))
            xn = jnp.where(last_row, nrow, pltpu.roll(x, Tb - 1, 0))
            outs.append(x + mup_ref[:, cs] * (xp - x) + mun_ref[:, cs] * (xn - x))
        return outs[0] if len(outs) == 1 else jnp.concatenate(outs, axis=-1)

    r = shifted(0, D)
    k = shifted(D, 2 * D)
    v = shifted(2 * D, 3 * D)
    wd = shifted(3 * D, 3 * D + 2 * R_DECAY)
    ad = shifted(3 * D + 2 * R_DECAY, 3 * D + 2 * R_DECAY + 2 * R_ICL)
    gd = shifted(3 * D + 2 * R_DECAY + 2 * R_ICL, 3 * D + 2 * R_DECAY + 2 * R_ICL + R_GATE)

    zw = w0_ref[...] + jnp.dot(jnp.tanh(wd).astype(BF16), w2_ref[...], preferred_element_type=F32)
    dec = jnp.exp(-EXP_NEG_HALF * _sig(zw))
    asig = _sig(a0_ref[...] + jnp.dot(ad.astype(BF16), a2_ref[...], preferred_element_type=F32))
    g = jnp.dot(_sig(gd).astype(BF16), g2_ref[...], preferred_element_type=F32)

    s0 = asig[:, :D]
    s1 = asig[:, D:]
    e = r * k * (2.0 + (s0 + s1 - 2.0) * ka_ref[...]) * rk_ref[...]
    parts = []
    for c in range(0, D, LANES):
        et = e[:, c:c + LANES]
        hi = et.astype(BF16)
        lo = (et - hi.astype(F32)).astype(BF16)
        parts.append(jnp.dot(hi, bd_ref[...], preferred_element_type=F32)
                     + jnp.dot(lo, bd_ref[...], preferred_element_type=F32))
    rk_b = jnp.concatenate(parts, axis=-1)

    def store_tiles(o_ref, val):
        for q in range(D // LANES):
            o_ref[q, 0] = val[:, q * LANES:(q + 1) * LANES]

    store_tiles(r_o, r)
    store_tiles(k_o, k)
    store_tiles(v_o, v)
    store_tiles(d0_o, dec[:, :D])
    store_tiles(d1_o, dec[:, D:])
    store_tiles(s0_o, s0)
    store_tiles(s1_o, s1)
    store_tiles(g_o, g)
    store_tiles(bon_o, rk_b * v)


def _rwkv_nat(p, mu_prev, mu_next, w0, w2, a0, a2, g2, k_a, r_k, *, Tb):
    B, S, NR = p.shape
    D = (NR - 2 * R_DECAY - 2 * R_ICL - R_GATE) // 3
    nh = Tb // 8
    last = S // 8 - 1
    zero = jnp.zeros((R_DECAY, D), F32)
    w2bd = jnp.concatenate([jnp.concatenate([w2[0], zero], 1), jnp.concatenate([zero, w2[1]], 1)], 0).astype(BF16)
    a2bd = jnp.concatenate([jnp.concatenate([a2[0], zero], 1), jnp.concatenate([zero, a2[1]], 1)], 0).astype(BF16)
    half = jnp.arange(LANES) // RWKV_HEAD
    bd = (half[:, None] == half[None, :]).astype(BF16)
    full = lambda shape: pl.BlockSpec(shape, lambda b, i: (0,) * len(shape))
    out_spec = pl.BlockSpec((D // LANES, 1, Tb, LANES), lambda b, i: (0, b, i, 0))
    out = jax.ShapeDtypeStruct((D // LANES, B, S, LANES), F32)
    return pl.pallas_call(
        functools.partial(_rwkv_nat_kernel, Tb=Tb, D=D),
        grid=(B, S // Tb),
        in_specs=[pl.BlockSpec((1, Tb, NR), lambda b, i: (b, i, 0)),
                  pl.BlockSpec((1, 8, NR), lambda b, i: (b, jnp.maximum(i * nh - 1, 0), 0)),
                  pl.BlockSpec((1, 8, NR), lambda b, i: (b, jnp.minimum((i + 1) * nh, last), 0)),
                  full((1, NR)), full((1, NR)),
                  full((1, 2 * D)), full((2 * R_DECAY, 2 * D)),
                  full((1, 2 * D)), full((2 * R_ICL, 2 * D)),
                  full((R_GATE, D)), full((1, D)), full((1, D)), full((LANES, LANES))],
        out_specs=[out_spec] * 9,
        out_shape=[out] * 9,
        compiler_params=_params(("arbitrary", "arbitrary")),
        name="rwkv_tokens",
    )(p, p, p, mu_prev[None], mu_next[None], w0.reshape(1, 2 * D), w2bd, a0.reshape(1, 2 * D), a2bd,
      g2.astype(BF16), k_a[None], r_k.reshape(1, D), bd)


SCAN_PLANES = 32
SCAN_ACC = 2
SCAN_GROUP = 4
SCAN_ROW_PAD = 8
OP_R, OP_K, OP_V, OP_A, OP_B = range(5)


def _scan_kernel(rf, rm, kf, km, vf, vm, df, dm, sf, sm_, kk_ref, ka_ref, y_ref, P_ref, ops_ref, in_ref, c_ref,
                 *, Tc):
    @pl.when(pl.program_id(0) == 0)
    def _():
        P_ref[...] = jnp.zeros(P_ref.shape, F32)

    c_ref[...] = jnp.ones(c_ref.shape, F32)

    HD = RWKV_HEAD
    half = LANES // 2
    zeros = jnp.zeros((HD, LANES), F32)
    kkT = kk_ref[...]
    kaT = ka_ref[...]
    srcs = (rf, rm, kf, km, vf, vm, df, dm, sf, sm_)
    RF, RM, KF, KM, VF, VM, DF, DM, SF, SM = range(len(srcs))
    RS = Tc + SCAN_ROW_PAD
    for qi, src in enumerate(srcs):
        for n in range(half):
            in_ref[qi, n * RS:n * RS + Tc, :] = src[n // src.shape[1], n % src.shape[1]]

    def stacked(f, m, s):
        top = in_ref[f, pl.ds(s, half, stride=RS), :]
        bot = in_ref[m, pl.ds(Tc - 1 - s, half, stride=RS), :]
        return jnp.concatenate([top, bot], axis=0).T

    def build(s, slot):
        kT = stacked(KF, KM, s)
        sT = stacked(SF, SM, s)
        kk = (kT * kkT).reshape(2, HD, LANES)
        nrm = jnp.sqrt(jnp.sum(kk * kk, axis=1, keepdims=True))
        kk = (kk / jnp.maximum(nrm, 1e-12)).reshape(LANES, LANES)
        c_prev = c_ref[...]
        c = c_prev * stacked(DF, DM, s)
        c_ref[...] = c
        c_inv = 1.0 / c
        ops_ref[slot, OP_R] = stacked(RF, RM, s) * c
        ops_ref[slot, OP_K] = kT * (1.0 + (sT - 1.0) * kaT) * c_inv
        ops_ref[slot, OP_V] = stacked(VF, VM, s)
        ops_ref[slot, OP_A] = -kk * c_prev
        ops_ref[slot, OP_B] = kk * sT * c_inv

    def sweep(s, slot, unscale):
        def op_row(idx, row):
            return ops_ref[slot, idx, pl.ds(row, 1), :]

        for g in range(LANES // HD):
            def sa_body(jb, sas):
                sas = list(sas)
                for jj in range(SCAN_PLANES):
                    row = g * HD + jb * SCAN_PLANES + jj
                    sas[jj % SCAN_ACC] = sas[jj % SCAN_ACC] + P_ref[row] * op_row(OP_A, row)
                return tuple(sas)

            sa = sum(lax.fori_loop(0, HD // SCAN_PLANES, sa_body, (zeros,) * SCAN_ACC))
            v = ops_ref[slot, OP_V, g * HD:(g + 1) * HD, :]

            def upd_body(jb, ys):
                ys = list(ys)
                for jj in range(SCAN_PLANES):
                    row = g * HD + jb * SCAN_PLANES + jj
                    pn = (P_ref[row] + sa * op_row(OP_B, row)) + v * op_row(OP_K, row)
                    ys[jj % SCAN_ACC] = ys[jj % SCAN_ACC] + pn * op_row(OP_R, row)
                    P_ref[row] = pn * c_ref[pl.ds(row, 1), :] if unscale else pn
                return tuple(ys)

            y_ref[s, g * HD:(g + 1) * HD, :] = sum(lax.fori_loop(0, HD // SCAN_PLANES, upd_body, (zeros,) * SCAN_ACC))

    def step_group(sg, last):
        s0 = sg * SCAN_GROUP
        for u in range(SCAN_GROUP):
            build(s0 + u, u)
        for u in range(SCAN_GROUP):
            sweep(s0 + u, u, last and u == SCAN_GROUP - 1)

    groups = Tc // SCAN_GROUP

    def group_body(sg, carry):
        step_group(sg, False)
        return carry

    lax.fori_loop(0, groups - 1, group_body, 0)
    step_group(groups - 1, True)


def _rwkv_scan(r, k, v, d0, d1, s0, s1, k_kT, k_aT, *, Tc):
    P, B, S, _ = r.shape
    nC = S // Tc
    fwd = pl.BlockSpec((P, B, Tc, LANES), lambda c: (0, 0, c, 0))
    mir = pl.BlockSpec((P, B, Tc, LANES), lambda c: (0, 0, nC - 1 - c, 0))
    tab = pl.BlockSpec((LANES, LANES), lambda c: (0, 0))
    return pl.pallas_call(
        functools.partial(_scan_kernel, Tc=Tc),
        grid=(nC,),
        in_specs=[fwd, mir, fwd, mir, fwd, mir, fwd, mir, fwd, mir, tab, tab],
        out_specs=pl.BlockSpec((Tc, LANES, LANES), lambda c: (c, 0, 0)),
        out_shape=jax.ShapeDtypeStruct((S, LANES, LANES), F32),
        scratch_shapes=[pltpu.VMEM((LANES, RWKV_HEAD, LANES), F32), pltpu.VMEM((SCAN_GROUP, 5, LANES, LANES), F32),
                        pltpu.VMEM((10, P * B * (Tc + SCAN_ROW_PAD), LANES), F32), pltpu.VMEM((LANES, LANES), F32)],
        compiler_params=_params(("arbitrary",)),
        name="rwkv_scan",
    )(r, r, k, k, v, v, d0, d1, s0, s1, k_kT, k_aT)


def _rwkv_post_kernel(yf_ref, ym_ref, bonf_ref, bonm_ref, gf_ref, gm_ref, lng_ref, lnb_ref, of_ref, om_ref, tok_ref,
                      *, Tb):
    half = LANES // 2

    def body(s, carry):
        y = yf_ref[s] + pltpu.roll(ym_ref[Tb - 1 - s], half, 1)
        y = y.reshape(2, RWKV_HEAD, LANES)
        mu = jnp.mean(y, axis=1, keepdims=True)
        d = y - mu
        var = jnp.mean(d * d, axis=1, keepdims=True)
        yn = (d * lax.rsqrt(var + RWKV_GN_EPS)).reshape(LANES, LANES) * lng_ref[...] + lnb_ref[...]
        tok = yn.T
        tok_ref[0, pl.ds(s, half, stride=Tb), :] = tok[:half]
        tok_ref[1, pl.ds(Tb - 1 - s, half, stride=Tb), :] = tok[half:]
        return carry

    lax.fori_loop(0, Tb, body, 0, unroll=2)
    nb = of_ref.shape[1]
    for n in range(half):
        p, b = n // nb, n % nb
        rows = slice(n * Tb, (n + 1) * Tb)
        of_ref[p, b] = (tok_ref[0, rows, :] + bonf_ref[p, b]) * gf_ref[p, b]
        om_ref[p, b] = (tok_ref[1, rows, :] + bonm_ref[p, b]) * gm_ref[p, b]


def _rwkv_post(yT, bonus, g, ln_gT, ln_bT, *, Tb):
    P, B, S, _ = bonus.shape
    nC = S // Tb
    nH = nC // 2
    fwd = pl.BlockSpec((P, B, Tb, LANES), lambda c: (0, 0, c, 0))
    mir = pl.BlockSpec((P, B, Tb, LANES), lambda c: (0, 0, nC - 1 - c, 0))
    tab = pl.BlockSpec((LANES, LANES), lambda c: (0, 0))
    out = jax.ShapeDtypeStruct((P, B, S // 2, LANES), F32)
    return pl.pallas_call(
        functools.partial(_rwkv_post_kernel, Tb=Tb),
        grid=(nH,),
        in_specs=[pl.BlockSpec((Tb, LANES, LANES), lambda c: (c, 0, 0)),
                  pl.BlockSpec((Tb, LANES, LANES), lambda c: (nC - 1 - c, 0, 0)),
                  fwd, mir, fwd, mir, tab, tab],
        out_specs=[fwd, pl.BlockSpec((P, B, Tb, LANES), lambda c: (0, 0, nH - 1 - c, 0))],
        out_shape=[out, out],
        scratch_shapes=[pltpu.VMEM((2, P * B * Tb, LANES), F32)],
        compiler_params=_params(("arbitrary",)),
        name="rwkv_groupnorm",
    )(yT, yT, bonus, bonus, g, g, ln_gT, ln_bT)


def _pair_table(p, batch):
    pairs = p.shape[0] // LANES
    t = jnp.repeat(p.reshape(pairs, LANES).T, batch, axis=1)
    return jnp.tile(t, (1, LANES // (pairs * batch)))


NA_GROUP = 4


def _na_kernel(q_ref, k_ref, v_ref, bias_ref, o_ref, *, rows, win):
    GW = GRID_W
    lane = lax.broadcasted_iota(jnp.int32, (2 * GW, LANES), 1)
    head = lax.broadcasted_iota(jnp.int32, (2 * GW, LANES), 0) // GW
    own = (lane // NA_HEAD) == head
    qc = lax.broadcasted_iota(jnp.int32, (2 * GW, win * GW), 0) % GW
    kc = lax.broadcasted_iota(jnp.int32, (2 * GW, win * GW), 1) % GW
    cs = jnp.clip(qc - NA_WIN_C // 2, 0, GW - NA_WIN_C)
    ok = (kc >= cs) & (kc < cs + NA_WIN_C)
    low = lax.broadcasted_iota(jnp.int32, (GW, LANES), 1) < NA_HEAD
    scale = NA_HEAD ** -0.5

    def body(rg, carry):
        scores, vbs, qss = [], [], []
        for i in range(NA_GROUP):
            r = rg * NA_GROUP + i
            r0 = jnp.clip(r - win // 2, 0, rows - win)
            qs = pl.multiple_of(r * GW, GW)
            ks = pl.multiple_of(r0 * GW, GW)
            q = q_ref[0, pl.ds(qs, GW), :] * scale
            qq = jnp.where(own, jnp.concatenate([q, q], axis=0), 0.0).astype(BF16)
            kb = k_ref[0, pl.ds(ks, win * GW), :]
            s = lax.dot_general(qq, kb, (((1,), (1,)), ((), ())), preferred_element_type=F32)
            scores.append(jnp.where(ok, s + bias_ref[0, r - r0], -1e30))
            vbs.append(v_ref[0, pl.ds(ks, win * GW), :])
            qss.append(qs)
        probs, sums = [], []
        for s in scores:
            p = jnp.exp(s - jnp.max(s, axis=-1, keepdims=True))
            sums.append(jnp.sum(p, axis=-1, keepdims=True))
            probs.append(p.astype(BF16))
        for p, l, vb, qs in zip(probs, sums, vbs, qss):
            o = jnp.dot(p, vb, preferred_element_type=F32) / l
            o_ref[0, pl.ds(qs, GW), :] = jnp.where(low, o[:GW], o[GW:]).astype(o_ref.dtype)
        return carry

    lax.fori_loop(0, rows // NA_GROUP, body, 0)


def _na_bias(rpb, win):
    H = rpb.shape[0]
    pad = GRID_W - NA_WIN_C
    rp = jnp.pad(rpb, ((0, 0), (0, 0), (pad, pad)), mode="edge")
    cols = jnp.stack([rp[:, :, GRID_W - 1 - c:2 * GRID_W - 1 - c] for c in range(GRID_W)], axis=2)
    t = jnp.stack([cols[:, NA_WIN_R_MAX - 1 - d:NA_WIN_R_MAX - 1 - d + win] for d in range(win)], axis=1)
    t = t.reshape(H // 2, 2, win, win, GRID_W, GRID_W).transpose(0, 2, 1, 4, 3, 5)
    return t.reshape(H // 2, win, 2 * GRID_W, win * GRID_W).astype(F32)


def _neighbourhood_attention(p, rpb):
    B, S, D3 = p.shape
    D = D3 // 3
    pairs = D // LANES
    rows = S // GRID_W
    win = min(NA_WIN_R_MAX, rows)
    bias = _na_bias(rpb, win)
    return pl.pallas_call(
        functools.partial(_na_kernel, rows=rows, win=win),
        grid=(B, pairs),
        in_specs=[pl.BlockSpec((1, S, LANES), lambda b, h: (b, 0, h)),
                  pl.BlockSpec((1, S, LANES), lambda b, h: (b, 0, pairs + h)),
                  pl.BlockSpec((1, S, LANES), lambda b, h: (b, 0, 2 * pairs + h)),
                  pl.BlockSpec((1, win, 2 * GRID_W, win * GRID_W), lambda b, h: (h, 0, 0, 0))],
        out_specs=pl.BlockSpec((1, S, LANES), lambda b, h: (b, 0, h)),
        out_shape=jax.ShapeDtypeStruct((B, S, D), BF16),
        compiler_params=_params(("arbitrary", "arbitrary")),
        name="neighbourhood_attention",
    )(p, p, p, bias)


def _merge_kernel(x_ref, xn_ref, zc_ref, yr_lo_ref, yr_hi_ref, yn_ref, wg_ref, bg_ref, wc_ref, bc_ref, wr_ref, wn_ref,
                  wo_ref, o_ref, *, seq):
    tm, D = x_ref.shape
    gate = lambda k: _sig(jnp.dot(xn_ref[...], wg_ref[:, k * D:(k + 1) * D], preferred_element_type=F32)
                          + bg_ref[:, k * D:(k + 1) * D])
    yc = jnp.dot(zc_ref[...].astype(BF16), wc_ref[...], preferred_element_type=F32) + bc_ref[...]
    second_half = (pl.program_id(0) * tm) % seq >= seq // 2
    tiles = lambda ref: jnp.concatenate([ref[q] for q in range(ref.shape[0])], axis=-1)
    yr_in = jnp.where(second_half, tiles(yr_hi_ref), tiles(yr_lo_ref))
    yr = jnp.dot(yr_in.astype(BF16), wr_ref[...], preferred_element_type=F32)
    yn = jnp.dot(yn_ref[...].astype(BF16), wn_ref[...], preferred_element_type=F32)
    merged = gate(0) * yc + gate(1) * yr + gate(2) * yn
    o_ref[...] = x_ref[...] + jnp.dot(merged.astype(BF16), wo_ref[...], preferred_element_type=F32)


def _merge(x2, xn, zc, yr_lo, yr_hi, yn, wg, bg, wc, bc, wr, wn, wo, *, tm, seq):
    M, D = x2.shape
    hs = seq // 2
    row = pl.BlockSpec((tm, D), lambda i: (i, 0))
    wsp = pl.BlockSpec((D, D), lambda i: (0, 0), pipeline_mode=pl.Buffered(1))
    half_rows = pl.BlockSpec((D // LANES, tm, LANES), lambda i: (0, (((i * tm) // seq) * hs + (i * tm) % hs) // tm, 0))
    return pl.pallas_call(
        functools.partial(_merge_kernel, seq=seq),
        grid=(M // tm,),
        in_specs=[row, row, row, half_rows, half_rows, row,
                  pl.BlockSpec((D, 3 * D), lambda i: (0, 0), pipeline_mode=pl.Buffered(1)),
                  pl.BlockSpec((1, 3 * D), lambda i: (0, 0)),
                  wsp, pl.BlockSpec((1, D), lambda i: (0, 0)), wsp, wsp, wsp],
        out_specs=row,
        out_shape=jax.ShapeDtypeStruct((M, D), F32),
        compiler_params=_params(("arbitrary",)),
        name="gated_merge",
    )(x2, xn, zc, yr_lo, yr_hi, yn, wg, bg, wc, bc, wr, wn, wo)


def _xattn_kernel(x_ref, g_ref, wq_ref, kv_ref, wo_ref, o_ref):
    x = x_ref[0]
    D = x.shape[-1]
    dh = D // XA_HEADS
    h = _rms(x, g_ref[...]).astype(BF16)
    q = jnp.dot(h, wq_ref[...], preferred_element_type=F32)
    outs = []
    for hd in range(XA_HEADS):
        qh = q[:, hd * dh:(hd + 1) * dh].astype(BF16)
        km = kv_ref[0, :, hd * dh:(hd + 1) * dh]
        vm = kv_ref[0, :, D + hd * dh:D + (hd + 1) * dh]
        s = lax.dot_general(qh, km, (((1,), (1,)), ((), ())), preferred_element_type=F32) * dh ** -0.5
        m = jnp.max(s, axis=-1, keepdims=True)
        p = jnp.exp(s - m)
        l = jnp.sum(p, axis=-1, keepdims=True)
        outs.append(jnp.dot(p.astype(BF16), vm, preferred_element_type=F32) / l)
    o = jnp.concatenate(outs, axis=-1).astype(BF16)
    o_ref[0] = x + jnp.dot(o, wo_ref[...], preferred_element_type=F32)


def _cross_attention(x, g, wq, kv, wo, *, tm):
    B, S, D = x.shape
    Mm = kv.shape[1]
    row = pl.BlockSpec((1, tm, D), lambda b, i: (b, i, 0))
    wsp = pl.BlockSpec((D, D), lambda b, i: (0, 0))
    return pl.pallas_call(
        _xattn_kernel,
        grid=(B, S // tm),
        in_specs=[row, pl.BlockSpec((1, D), lambda b, i: (0, 0)), wsp,
                  pl.BlockSpec((1, Mm, 2 * D), lambda b, i: (b, 0, 0)), wsp],
        out_specs=row,
        out_shape=jax.ShapeDtypeStruct((B, S, D), F32),
        compiler_params=_params(("arbitrary", "arbitrary")),
        name="memory_cross_attention",
    )(x, g, wq, kv, wo)


def _mlp_kernel(x_ref, g_ref, w1_ref, w2_ref, gf_ref, o_ref, xn_ref, acc_ref, *, final_norm):
    f = pl.program_id(1)

    @pl.when(f == 0)
    def _():
        xn_ref[...] = _rms(x_ref[...], g_ref[...]).astype(BF16)
        acc_ref[...] = jnp.zeros(acc_ref.shape, F32)

    h = jnp.maximum(jnp.dot(xn_ref[...], w1_ref[...], preferred_element_type=F32), 0.0)
    acc_ref[...] += jnp.dot((h * h).astype(BF16), w2_ref[...], preferred_element_type=F32)

    @pl.when(f == pl.num_programs(1) - 1)
    def _():
        y = x_ref[...] + acc_ref[...]
        if final_norm:
            y = _rms(y, gf_ref[...])
        o_ref[...] = y


def _mlp(x2, g, w1, w2, gf, *, tm, tf, final_norm):
    M, D = x2.shape
    F = w1.shape[1]
    row = pl.BlockSpec((tm, D), lambda i, f: (i, 0))
    vec = pl.BlockSpec((1, D), lambda i, f: (0, 0))
    return pl.pallas_call(
        functools.partial(_mlp_kernel, final_norm=final_norm),
        grid=(M // tm, F // tf),
        in_specs=[row, vec, pl.BlockSpec((D, tf), lambda i, f: (0, f), **_resident(tf == F)),
                  pl.BlockSpec((tf, D), lambda i, f: (f, 0), **_resident(tf == F)), vec],
        out_specs=row,
        out_shape=jax.ShapeDtypeStruct((M, D), F32),
        scratch_shapes=[pltpu.VMEM((tm, D), BF16), pltpu.VMEM((tm, D), F32)],
        compiler_params=_params(("arbitrary", "arbitrary")),
        name="relu2_mlp",
    )(x2, g, w1, w2, gf)


def _tiles(S):
    return dict(
        proj_tm=512,
        side_tm=2048,
        side_tn=1024,
        conv_ts=min(512, S),
        nat_tb=min(128, S),
        post_tb=16,
        scan_tc=16,
        merge_tm=min(512, S // 2),
        xa_tm=min(512, S),
        mlp_tm=512,
    )


def kernel(x, mem, norm_mix_g, w_in, gate_b, conv_b_glu, conv_dw_w, conv_dw_b, conv_ln_g, conv_ln_b,
           conv_proj_w, conv_proj_b, rwkv_mu_prev, rwkv_mu_next, rwkv_w0, rwkv_w2, rwkv_a0, rwkv_a2,
           rwkv_g2, rwkv_k_k, rwkv_k_a, rwkv_r_k, rwkv_ln_g, rwkv_ln_b, rwkv_proj_w, na_rpb, na_proj_w,
           w_out, norm_xa_g, norm_mem_g, xa_wq, xa_wkv, xa_wo, norm_mlp_g, mlp_w1, mlp_w2, norm_f_g):
    B, S, D = x.shape
    depth = w_in.shape[0]
    Mm = mem.shape[1]
    T = _tiles(S)
    n_conv = 2 * D
    n_rwkv = 3 * D + 2 * R_DECAY + 2 * R_ICL + R_GATE
    n_na = 3 * D
    off_rwkv = n_conv
    off_na = off_rwkv + n_rwkv
    off_gate = off_na + n_na

    x2 = x.reshape(B * S, D)
    mem2 = mem.reshape(B * Mm, D)
    for l in range(depth):
        w_in_bf = w_in[l].astype(BF16)
        g_mix = norm_mix_g[l][None]

        p_rwkv, xn = _norm_matmul(x2, g_mix, w_in_bf[:, off_rwkv:off_na], jnp.zeros((1, n_rwkv), F32),
                                  tm=T["proj_tm"], tn=n_rwkv, emit_xn=True, name="proj_rwkv")
        p_rwkv = p_rwkv.reshape(B, S, n_rwkv)

        def proj(c0, c1, name, bias=None, act=None, out_dtype=F32):
            n = c1 - c0
            b = jnp.zeros((1, n), F32) if bias is None else bias[None]
            return _matmul(xn, w_in_bf[:, c0:c1], b, tm=T["side_tm"], tn=T["side_tn"], act=act,
                           out_dtype=out_dtype, name=name)

        p_conv = proj(0, off_rwkv, "proj_conv").reshape(B, S, n_conv)
        p_na = proj(off_na, off_gate, "proj_na", out_dtype=BF16).reshape(B, S, n_na)

        zc = _conformer_conv(p_conv, conv_b_glu[l], conv_dw_w[l], conv_dw_b[l], conv_ln_g[l], conv_ln_b[l],
                             ts=T["conv_ts"])

        r, k, v, d0, d1, s0, s1, g, bonus = _rwkv_nat(
            p_rwkv, rwkv_mu_prev[l], rwkv_mu_next[l], rwkv_w0[l], rwkv_w2[l], rwkv_a0[l], rwkv_a2[l],
            rwkv_g2[l], rwkv_k_a[l], rwkv_r_k[l], Tb=T["nat_tb"])
        yT = _rwkv_scan(r, k, v, d0, d1, s0, s1, _pair_table(rwkv_k_k[l], B), _pair_table(rwkv_k_a[l], B),
                        Tc=T["scan_tc"])
        yr_lo, yr_hi = _rwkv_post(yT, bonus, g, _pair_table(rwkv_ln_g[l], B), _pair_table(rwkv_ln_b[l], B),
                                  Tb=T["post_tb"])

        y_na = _neighbourhood_attention(p_na, na_rpb[l])

        half_tiles = lambda a: a.reshape(D // LANES, B * S // 2, LANES)
        x2 = _merge(x2, xn, zc.reshape(B * S, D), half_tiles(yr_lo), half_tiles(yr_hi), y_na.reshape(B * S, D),
                    w_in_bf[:, off_gate:], gate_b[l][None], conv_proj_w[l].astype(BF16), conv_proj_b[l][None],
                    rwkv_proj_w[l].astype(BF16), na_proj_w[l].astype(BF16), w_out[l].astype(BF16),
                    tm=T["merge_tm"], seq=S)

        kv = _norm_matmul(mem2, norm_mem_g[l][None], xa_wkv[l].astype(BF16), jnp.zeros((1, 2 * D), F32),
                          tm=min(512, B * Mm), tn=1024, out_dtype=BF16, name="proj_mem_kv")
        xa = _cross_attention(x2.reshape(B, S, D), norm_xa_g[l][None], xa_wq[l].astype(BF16),
                              kv.reshape(B, Mm, 2 * D), xa_wo[l].astype(BF16), tm=T["xa_tm"])

        x2 = _mlp(xa.reshape(B * S, D), norm_mlp_g[l][None], mlp_w1[l].astype(BF16), mlp_w2[l].astype(BF16),
                  norm_f_g[None], tm=T["mlp_tm"], tf=mlp_w1.shape[-1], final_norm=(l == depth - 1))
    return x2.reshape(B, S, D)
```
